```python
import numpy as np
import jax
import jax.numpy as jnp
from jax import lax

D_MODEL = 1024
BATCH = 2
SEQ = 8192
DEPTH = 4
DEC_BATCH = 128
DEC_SEQ = 4
PAST_LEN = 8192
PAGE_SIZE = 128

A_HEADS = 8
A_KV_HEADS = 4
A_HEAD_DIM = 64
A_GROUP = A_HEADS // A_KV_HEADS
MOBA_BLOCK = 256
MOBA_TOPK = 3
B_HEADS = 8
MLA_Q_RANK = 384
MLA_KV_RANK = 256
MLA_D_NOPE = 64
MLA_D_ROPE = 32
MLA_D_V = 64
SWA_CONFIGS = ((128, 1), (512, 4), (2048, 16))
C_HEADS_PER_GROUP = 4
C_HEAD_DIM = 128
N_MEM = 256
MEM_HEADS = 4
MEM_HEAD_DIM = 128
N_EXPERTS = 64
N_EXPERT_GROUPS = 8
TOPK_GROUPS = 4
MOE_TOPK = 8
D_EXPERT = 256
D_SHARED = 256
ROUTED_SCALE = 2.5

ROPE_THETA = 10000.0
LN_EPS = 1e-5
RMS_EPS = 1e-6
NEG_INF = -1e30
Q_BLOCK = 128
DEEPNORM_ALPHA = (2 * DEPTH) ** 0.25
DEEPNORM_BETA = (8 * DEPTH) ** -0.25
N_EVEN = (DEPTH + 1) // 2
N_ODD = DEPTH // 2

A_Q = A_HEADS * A_HEAD_DIM
A_KV = A_KV_HEADS * A_HEAD_DIM
EVEN_IN_SIZES = (A_Q, A_KV, A_KV, MLA_Q_RANK, MLA_KV_RANK, MLA_D_ROPE)
EVEN_IN = sum(EVEN_IN_SIZES)
EVEN_OUT = A_Q + B_HEADS * MLA_D_V
N_SWA = len(SWA_CONFIGS)
ODD_IN = N_SWA * 3 * C_HEADS_PER_GROUP * C_HEAD_DIM
ODD_OUT = C_HEADS_PER_GROUP * C_HEAD_DIM
MEM_INNER = MEM_HEADS * MEM_HEAD_DIM
MLA_ROW = MLA_KV_RANK + MLA_D_ROPE

kernel_name = 'hybrid_moba_mla_dilated_moe_decoder_step'

F32 = jnp.float32


def _split(x, sizes):
    return jnp.split(x, [int(i) for i in np.cumsum(sizes)[:-1]], axis=-1)


def _rope(x, pos):
    half = x.shape[-1] // 2
    inv = ROPE_THETA ** (-jnp.arange(half, dtype=F32) / half)
    ang = pos.astype(F32)[:, None] * inv[None, :]
    cos = jnp.cos(ang)[:, None, :]
    sin = jnp.sin(ang)[:, None, :]
    xf = x.astype(F32)
    x1, x2 = xf[..., :half], xf[..., half:]
    return jnp.concatenate([x1 * cos - x2 * sin, x2 * cos + x1 * sin], -1).astype(x.dtype)


def _layer_norm(x, g, b):
    xf = x.astype(F32)
    mu = xf.mean(-1, keepdims=True)
    var = jnp.square(xf - mu).mean(-1, keepdims=True)
    return ((xf - mu) * lax.rsqrt(var + LN_EPS) * g + b).astype(x.dtype)


def _rms_norm(x, g):
    xf = x.astype(F32)
    return (xf * lax.rsqrt(jnp.square(xf).mean(-1, keepdims=True) + RMS_EPS) * g).astype(x.dtype)


def _attend(parts):
    probs = jax.nn.softmax(jnp.concatenate([s for s, _, _ in parts], -1), axis=-1)
    offs = np.cumsum([0] + [s.shape[-1] for s, _, _ in parts])
    out = None
    for (s, v, spec), a, b in zip(parts, offs[:-1], offs[1:]):
        term = jnp.einsum(spec, probs[..., int(a):int(b)].astype(v.dtype), v)
        out = term if out is None else out + term
    return out


def _lse_attend(s, v, spec):
    lse = jax.nn.logsumexp(s, axis=-1)
    p = jnp.exp(s - lse[..., None])
    return jnp.einsum(spec, p.astype(v.dtype), v), lse


def _even_project(x, pos, w_in, q_norm, w_uq, kv_norm):
    bsz, s = x.shape[:2]
    qa, ka, va, cq, ckv, kpe = _split(x @ w_in, EVEN_IN_SIZES)
    qa = _rope(qa.reshape(bsz, s, A_HEADS, A_HEAD_DIM), pos)
    ka = _rope(ka.reshape(bsz, s, A_KV_HEADS, A_HEAD_DIM), pos)
    va = va.reshape(bsz, s, A_KV_HEADS, A_HEAD_DIM)
    qb = (_rms_norm(cq, q_norm) @ w_uq).reshape(bsz, s, B_HEADS, MLA_D_NOPE + MLA_D_ROPE)
    q_nope = qb[..., :MLA_D_NOPE]
    q_pe = _rope(qb[..., MLA_D_NOPE:], pos)
    ckv = _rms_norm(ckv, kv_norm)
    kpe = _rope(kpe[:, :, None, :], pos)[:, :, 0]
    return qa, ka, va, q_nope, q_pe, ckv, kpe


def _moba_prompt(q, k, v):
    bsz, seq = q.shape[:2]
    n_blk = -(-seq // MOBA_BLOCK)
    pad = n_blk * MOBA_BLOCK - seq
    kp = jnp.pad(k, ((0, 0), (0, pad), (0, 0), (0, 0)))
    vp = jnp.pad(v, ((0, 0), (0, pad), (0, 0), (0, 0)))
    kb = kp.reshape(bsz, n_blk, MOBA_BLOCK, A_KV_HEADS, A_HEAD_DIM).transpose(0, 3, 1, 2, 4)
    vb = vp.reshape(bsz, n_blk, MOBA_BLOCK, A_KV_HEADS, A_HEAD_DIM).transpose(0, 3, 1, 2, 4)
    kmean = kb.astype(F32).mean(3)
    n_sel = min(MOBA_TOPK, n_blk)
    scale = A_HEAD_DIM ** -0.5
    n_chunk = seq // Q_BLOCK
    qc = q.reshape(bsz, n_chunk, Q_BLOCK, A_KV_HEADS, A_GROUP, A_HEAD_DIM).swapaxes(0, 1)
    bi = jnp.arange(bsz)[:, None, None, None, None]
    ki = jnp.arange(A_KV_HEADS)[None, None, :, None, None]
    blk_ids = jnp.arange(n_blk)
    offs = jnp.arange(MOBA_BLOCK)
    qoff = jnp.arange(Q_BLOCK)

    def chunk(args):
        c, qb = args
        start = c * Q_BLOCK
        own = start // MOBA_BLOCK
        qpos = start + qoff
        gate = jnp.einsum('bqkgd,bknd->bqkgn', qb.astype(F32), kmean)
        gate = jnp.where(blk_ids < own, gate, NEG_INF)
        idx = lax.top_k(gate, n_sel)[1]
        valid = (idx < own)[..., None]
        ks = kb[bi, ki, idx]
        vs = vb[bi, ki, idx]
        s_sel = jnp.einsum('bqkgd,bqkgnld->bqkgnl', qb, ks).astype(F32) * scale
        s_sel = jnp.where(valid, s_sel, NEG_INF).reshape(bsz, Q_BLOCK, A_KV_HEADS, A_GROUP, n_sel * MOBA_BLOCK)
        vs = vs.reshape(bsz, Q_BLOCK, A_KV_HEADS, A_GROUP, n_sel * MOBA_BLOCK, A_HEAD_DIM)
        k_own = lax.dynamic_slice_in_dim(kp, own * MOBA_BLOCK, MOBA_BLOCK, axis=1)
        v_own = lax.dynamic_slice_in_dim(vp, own * MOBA_BLOCK, MOBA_BLOCK, axis=1)
        s_own = jnp.einsum('bqkgd,blkd->bqkgl', qb, k_own).astype(F32) * scale
        causal = (own * MOBA_BLOCK + offs)[None, :] <= qpos[:, None]
        s_own = jnp.where(causal[None, :, None, None, :], s_own, NEG_INF)
        o = _attend([(s_sel, vs, 'bqkgl,bqkgld->bqkgd'), (s_own, v_own, 'bqkgl,blkd->bqkgd')])
        return o.reshape(bsz, Q_BLOCK, A_HEADS, A_HEAD_DIM)

    out = lax.map(chunk, (jnp.arange(n_chunk), qc))
    return out.swapaxes(0, 1).reshape(bsz, seq, A_HEADS, A_HEAD_DIM)


def _moba_sample(q, k_new, v_new, pool_k, pool_v, li, page_table):
    dbs, t = q.shape[:2]
    ppb = MOBA_BLOCK // PAGE_SIZE
    n_full = PAST_LEN // MOBA_BLOCK
    n_part_pages = PAST_LEN // PAGE_SIZE - n_full * ppb
    scale = A_HEAD_DIM ** -0.5
    qg = q.reshape(dbs, t, A_KV_HEADS, A_GROUP, A_HEAD_DIM)
    parts = []
    if n_full > 0:
        kfull = pool_k[li, page_table[:, :n_full * ppb]]
        kmean = kfull.astype(F32).reshape(dbs, n_full, MOBA_BLOCK, A_KV_HEADS, A_HEAD_DIM).mean(2)
        gate = jnp.einsum('btkgd,bnkd->btkgn', qg.astype(F32), kmean)
        n_sel = min(MOBA_TOPK, n_full)
        idx = lax.top_k(gate, n_sel)[1]
        lp = idx[..., None] * ppb + jnp.arange(ppb)
        bi = jnp.arange(dbs)[:, None, None, None, None, None]
        pg = page_table[bi, lp]
        ki = jnp.arange(A_KV_HEADS)[None, None, :, None, None, None]
        ks = pool_k[li, pg, :, ki].reshape(dbs, t, A_KV_HEADS, A_GROUP, n_sel * MOBA_BLOCK, A_HEAD_DIM)
        vs = pool_v[li, pg, :, ki].reshape(dbs, t, A_KV_HEADS, A_GROUP, n_sel * MOBA_BLOCK, A_HEAD_DIM)
        s_sel = jnp.einsum('btkgd,btkgld->btkgl', qg, ks).astype(F32) * scale
        parts.append((s_sel, vs, 'btkgl,btkgld->btkgd'))
    if n_part_pages > 0:
        part_pt = page_table[:, n_full * ppb:]
        kpart = pool_k[li, part_pt].reshape(dbs, n_part_pages * PAGE_SIZE, A_KV_HEADS, A_HEAD_DIM)
        vpart = pool_v[li, part_pt].reshape(dbs, n_part_pages * PAGE_SIZE, A_KV_HEADS, A_HEAD_DIM)
        s_part = jnp.einsum('btkgd,blkd->btkgl', qg, kpart).astype(F32) * scale
        parts.append((s_part, vpart, 'btkgl,blkd->btkgd'))
    s_new = jnp.einsum('btkgd,bskd->btkgs', qg, k_new).astype(F32) * scale
    causal = jnp.arange(t)[None, :] <= jnp.arange(t)[:, None]
    s_new = jnp.where(causal[None, :, None, None, :], s_new, NEG_INF)
    parts.append((s_new, v_new, 'btkgs,bskd->btkgd'))
    return _attend(parts).reshape(dbs, t, A_HEADS, A_HEAD_DIM)


def _dense_causal(q, k, v, scale):
    bsz, seq, nh = q.shape[:3]
    n_chunk = seq // Q_BLOCK
    qc = q.reshape(bsz, n_chunk, Q_BLOCK, nh, q.shape[-1]).swapaxes(0, 1)
    kpos = jnp.arange(seq)
    qoff = jnp.arange(Q_BLOCK)

    def chunk(args):
        c, qb = args
        s = jnp.einsum('bqhd,bkhd->bhqk', qb, k).astype(F32) * scale
        s = jnp.where(kpos[None, :] <= (c * Q_BLOCK + qoff)[:, None], s, NEG_INF)
        p = jax.nn.softmax(s, axis=-1)
        return jnp.einsum('bhqk,bkhd->bqhd', p.astype(v.dtype), v)

    out = lax.map(chunk, (jnp.arange(n_chunk), qc))
    return out.swapaxes(0, 1).reshape(bsz, seq, nh, v.shape[-1])


def _mla_prompt(q_nope, q_pe, ckv, kpe, w_uk, w_uv):
    k_nope = jnp.einsum('bsc,chd->bshd', ckv, w_uk)
    v = jnp.einsum('bsc,chd->bshd', ckv, w_uv)
    q = jnp.concatenate([q_nope, q_pe], -1)
    k = jnp.concatenate([k_nope, jnp.broadcast_to(kpe[:, :, None, :], k_nope.shape[:3] + (MLA_D_ROPE,))], -1)
    return _dense_causal(q, k, v, (MLA_D_NOPE + MLA_D_ROPE) ** -0.5)


def _mla_sample(q_nope, q_pe, ckv, kpe, pool, li, page_table, w_uk, w_uv):
    dbs, t = q_nope.shape[:2]
    scale = (MLA_D_NOPE + MLA_D_ROPE) ** -0.5
    past = pool[li, page_table].reshape(dbs, -1, MLA_ROW)
    c_past, pe_past = past[..., :MLA_KV_RANK], past[..., MLA_KV_RANK:]
    q_lat = jnp.einsum('bthd,chd->bthc', q_nope, w_uk)
    s_past = (jnp.einsum('bthc,bpc->bthp', q_lat, c_past) + jnp.einsum('bthr,bpr->bthp', q_pe, pe_past)).astype(F32) * scale
    s_new = (jnp.einsum('bthc,bsc->bths', q_lat, ckv) + jnp.einsum('bthr,bsr->bths', q_pe, kpe)).astype(F32) * scale
    causal = jnp.arange(t)[None, :] <= jnp.arange(t)[:, None]
    s_new = jnp.where(causal[None, :, None, :], s_new, NEG_INF)
    o_lat = _attend([(s_past, c_past, 'bthp,bpc->bthc'), (s_new, ckv, 'bths,bsc->bthc')])
    return jnp.einsum('bthc,chd->bthd', o_lat, w_uv)


def _odd_project(x, pos, w_in):
    bsz, s = x.shape[:2]
    h = (x @ w_in).reshape(bsz, s, N_SWA, 3, C_HEADS_PER_GROUP, C_HEAD_DIM)
    return [(_rope(h[:, :, g, 0], pos), _rope(h[:, :, g, 1], pos), h[:, :, g, 2]) for g in range(N_SWA)]


def _dilated_prompt(q, k, v, window, dil):
    bsz, seq = q.shape[:2]
    n_key = window // dil + 1
    kp = jnp.pad(k, ((0, 0), (window, 0), (0, 0), (0, 0)))
    vp = jnp.pad(v, ((0, 0), (window, 0), (0, 0), (0, 0)))
    jd = jnp.arange(n_key) * dil
    qoff = jnp.arange(Q_BLOCK)
    rel = qoff[:, None] + window - jd[None, :]
    n_chunk = seq // Q_BLOCK
    qc = q.reshape(bsz, n_chunk, Q_BLOCK, C_HEADS_PER_GROUP, C_HEAD_DIM).swapaxes(0, 1)
    scale = C_HEAD_DIM ** -0.5

    def chunk(args):
        c, qb = args
        start = c * Q_BLOCK
        ks = lax.dynamic_slice_in_dim(kp, start, Q_BLOCK + window, axis=1)[:, rel]
        vs = lax.dynamic_slice_in_dim(vp, start, Q_BLOCK + window, axis=1)[:, rel]
        valid = (start + qoff)[:, None] - jd[None, :] >= 0
        s = jnp.einsum('bqhd,bqnhd->bqhn', qb, ks).astype(F32) * scale
        s = jnp.where(valid[None, :, None, :], s, NEG_INF)
        return _lse_attend(s, vs, 'bqhn,bqnhd->bqhd')

    o, lse = lax.map(chunk, (jnp.arange(n_chunk), qc))
    return (o.swapaxes(0, 1).reshape(bsz, seq, C_HEADS_PER_GROUP, C_HEAD_DIM),
            lse.swapaxes(0, 1).reshape(bsz, seq, C_HEADS_PER_GROUP))


def _dilated_sample(q, k_new, v_new, buf, window, dil):
    t = q.shape[1]
    wb = buf.shape[1]
    n_key = window // dil + 1
    jd = jnp.arange(n_key) * dil
    kall = jnp.concatenate([buf[:, :, 0], k_new], 1)
    vall = jnp.concatenate([buf[:, :, 1], v_new], 1)
    tq = jnp.arange(t)
    idx = wb + tq[:, None] - jd[None, :]
    valid = (PAST_LEN + tq)[:, None] - jd[None, :] >= 0
    idx = jnp.maximum(idx, 0)
    ks, vs = kall[:, idx], vall[:, idx]
    s = jnp.einsum('bqhd,bqnhd->bqhn', q, ks).astype(F32) * (C_HEAD_DIM ** -0.5)
    s = jnp.where(valid[None, :, None, :], s, NEG_INF)
    return _lse_attend(s, vs, 'bqhn,bqnhd->bqhd')


def _combine_dilations(outs, lses):
    w = jax.nn.softmax(jnp.stack(lses, 0), axis=0)
    out = None
    for g in range(len(outs)):
        term = w[g][..., None].astype(outs[g].dtype) * outs[g]
        out = term if out is None else out + term
    return out


def _mem_kv(mem, w_kv):
    bsz = mem.shape[0]
    return (mem @ w_kv).reshape(bsz, N_MEM, 2, MEM_HEADS, MEM_HEAD_DIM)


def _mem_attn(x, kv, w_q, w_o):
    bsz, s = x.shape[:2]
    q = (x @ w_q).reshape(bsz, s, MEM_HEADS, MEM_HEAD_DIM)
    sc = jnp.einsum('bshd,bmhd->bshm', q, kv[:, :, 0]).astype(F32) * (MEM_HEAD_DIM ** -0.5)
    p = jax.nn.softmax(sc, axis=-1)
    o = jnp.einsum('bshm,bmhd->bshd', p.astype(kv.dtype), kv[:, :, 1])
    return o.reshape(bsz, s, MEM_INNER) @ w_o


def _moe(x, w_router, r_bias, w_g, w_u, w_d, ws_g, ws_u, ws_d):
    n = x.shape[0]
    s = jax.nn.sigmoid((x @ w_router).astype(F32))
    sb = s + r_bias.astype(F32)
    per_grp = N_EXPERTS // N_EXPERT_GROUPS
    gscore = lax.top_k(sb.reshape(n, N_EXPERT_GROUPS, per_grp), 2)[0].sum(-1)
    gidx = lax.top_k(gscore, TOPK_GROUPS)[1]
    gmask = jax.nn.one_hot(gidx, N_EXPERT_GROUPS, dtype=F32).sum(-2) > 0
    emask = jnp.repeat(gmask, per_grp, axis=-1)
    eidx = lax.top_k(jnp.where(emask, sb, NEG_INF), MOE_TOPK)[1]
    w = jnp.take_along_axis(s, eidx, axis=-1)
    w = w / w.sum(-1, keepdims=True) * ROUTED_SCALE
    gate = jnp.einsum('nk,nke->ne', w, jax.nn.one_hot(eidx, N_EXPERTS, dtype=F32))
    h = jax.nn.silu(jnp.einsum('nd,edf->nef', x, w_g)) * jnp.einsum('nd,edf->nef', x, w_u)
    y = jnp.einsum('nef,efd->nd', h * gate[:, :, None].astype(h.dtype), w_d)
    shared = (jax.nn.silu(x @ ws_g) * (x @ ws_u)) @ ws_d
    return y + shared


def setup_inputs(seed: int = 0) -> dict:
    key = jax.random.key(seed)
    keys = iter(jax.random.split(key, 40))

    def nrm(shape, scale=1.0):
        return jax.random.normal(next(keys), shape, F32) * scale

    n_pages = PAST_LEN // PAGE_SIZE
    n_pool = (5 * DEC_BATCH * n_pages) // 4
    page_table = jax.random.permutation(next(keys), n_pool)[:DEC_BATCH * n_pages].reshape(DEC_BATCH, n_pages).astype(jnp.int32)
    x_prompt = nrm((BATCH, SEQ, D_MODEL))
    x_sample = nrm((DEC_BATCH, DEC_SEQ, D_MODEL))
    mem_prompt = nrm((BATCH, N_MEM, D_MODEL))
    cache_moba_k = nrm((N_EVEN, n_pool, PAGE_SIZE, A_KV_HEADS, A_HEAD_DIM))
    cache_moba_v = nrm((N_EVEN, n_pool, PAGE_SIZE, A_KV_HEADS, A_HEAD_DIM))
    cache_mla = nrm((N_EVEN, n_pool, PAGE_SIZE, MLA_ROW))
    swa = [nrm((N_ODD, DEC_BATCH, min(w, PAST_LEN), 2, C_HEADS_PER_GROUP, C_HEAD_DIM)) for w, _ in SWA_CONFIGS]
    cache_mem_kv = nrm((DEPTH, DEC_BATCH, N_MEM, 2, MEM_HEADS, MEM_HEAD_DIM))
    ln_gain = 1.0 + nrm((DEPTH, 3, D_MODEL), 0.02)
    ln_bias = nrm((DEPTH, 3, D_MODEL), 0.02)
    w_in_even = nrm((N_EVEN, D_MODEL, EVEN_IN), D_MODEL ** -0.5)
    w_out_even = nrm((N_EVEN, EVEN_OUT, D_MODEL), DEEPNORM_BETA * EVEN_OUT ** -0.5)
    mla_q_norm = 1.0 + nrm((N_EVEN, MLA_Q_RANK), 0.02)
    mla_w_uq = nrm((N_EVEN, MLA_Q_RANK, B_HEADS * (MLA_D_NOPE + MLA_D_ROPE)), MLA_Q_RANK ** -0.5)
    mla_kv_norm = 1.0 + nrm((N_EVEN, MLA_KV_RANK), 0.02)
    mla_w_uk = nrm((N_EVEN, MLA_KV_RANK, B_HEADS, MLA_D_NOPE), MLA_KV_RANK ** -0.5)
    mla_w_uv = nrm((N_EVEN, MLA_KV_RANK, B_HEADS, MLA_D_V), MLA_KV_RANK ** -0.5)
    w_in_odd = nrm((N_ODD, D_MODEL, ODD_IN), D_MODEL ** -0.5)
    w_out_odd = nrm((N_ODD, ODD_OUT, D_MODEL), DEEPNORM_BETA * ODD_OUT ** -0.5)
    w_mem_q = nrm((DEPTH, D_MODEL, MEM_INNER), D_MODEL ** -0.5)
    w_mem_kv = nrm((DEPTH, D_MODEL, 2 * MEM_INNER), D_MODEL ** -0.5)
    w_mem_o = nrm((DEPTH, MEM_INNER, D_MODEL), DEEPNORM_BETA * MEM_INNER ** -0.5)
    w_router = nrm((DEPTH, D_MODEL, N_EXPERTS), D_MODEL ** -0.5)
    router_bias = nrm((DEPTH, N_EXPERTS), 0.01)
    w_exp_gate = nrm((DEPTH, N_EXPERTS, D_MODEL, D_EXPERT), D_MODEL ** -0.5)
    w_exp_up = nrm((DEPTH, N_EXPERTS, D_MODEL, D_EXPERT), D_MODEL ** -0.5)
    w_exp_down = nrm((DEPTH, N_EXPERTS, D_EXPERT, D_MODEL), DEEPNORM_BETA * D_EXPERT ** -0.5)
    w_sh_gate = nrm((DEPTH, D_MODEL, D_SHARED), D_MODEL ** -0.5)
    w_sh_up = nrm((DEPTH, D_MODEL, D_SHARED), D_MODEL ** -0.5)
    w_sh_down = nrm((DEPTH, D_SHARED, D_MODEL), DEEPNORM_BETA * D_SHARED ** -0.5)
    return {
        'x_prompt': x_prompt, 'x_sample': x_sample, 'mem_prompt': mem_prompt,
        'cache_moba_k': cache_moba_k, 'cache_moba_v': cache_moba_v, 'cache_mla': cache_mla,
        'cache_swa0_kv': swa[0], 'cache_swa1_kv': swa[1], 'cache_swa2_kv': swa[2],
        'cache_mem_kv': cache_mem_kv, 'page_table': page_table,
        'ln_gain': ln_gain, 'ln_bias': ln_bias,
        'w_in_even': w_in_even, 'w_out_even': w_out_even,
        'mla_q_norm': mla_q_norm, 'mla_w_uq': mla_w_uq, 'mla_kv_norm': mla_kv_norm,
        'mla_w_uk': mla_w_uk, 'mla_w_uv': mla_w_uv,
        'w_in_odd': w_in_odd, 'w_out_odd': w_out_odd,
        'w_mem_q': w_mem_q, 'w_mem_kv': w_mem_kv, 'w_mem_o': w_mem_o,
        'w_router': w_router, 'router_bias': router_bias,
        'w_exp_gate': w_exp_gate, 'w_exp_up': w_exp_up, 'w_exp_down': w_exp_down,
        'w_sh_gate': w_sh_gate, 'w_sh_up': w_sh_up, 'w_sh_down': w_sh_down,
    }


def reference(x_prompt, x_sample, mem_prompt, cache_moba_k, cache_moba_v, cache_mla,
              cache_swa0_kv, cache_swa1_kv, cache_swa2_kv, cache_mem_kv, page_table,
              ln_gain, ln_bias, w_in_even, w_out_even, mla_q_norm, mla_w_uq, mla_kv_norm,
              mla_w_uk, mla_w_uv, w_in_odd, w_out_odd, w_mem_q, w_mem_kv, w_mem_o,
              w_router, router_bias, w_exp_gate, w_exp_up, w_exp_down,
              w_sh_gate, w_sh_up, w_sh_down):
    bp, sp = x_prompt.shape[:2]
    bs, ss = x_sample.shape[:2]
    pos_p = jnp.arange(sp)
    pos_s = PAST_LEN + jnp.arange(ss)
    swa_caches = (cache_swa0_kv, cache_swa1_kv, cache_swa2_kv)
    moba_k_p, moba_v_p, mla_p, mem_p = [], [], [], []
    moba_k_s, moba_v_s, mla_s = [], [], []
    swa_p = [[] for _ in range(N_SWA)]
    swa_s = [[] for _ in range(N_SWA)]
    xp, xs = x_prompt, x_sample
    for l in range(DEPTH):
        if l % 2 == 0:
            e = l // 2
            pw = (w_in_even[e], mla_q_norm[e], mla_w_uq[e], mla_kv_norm[e])
            qa, ka, va, qn, qpe, ckv, kpe = _even_project(xp, pos_p, *pw)
            heads = jnp.concatenate([
                _moba_prompt(qa, ka, va).reshape(bp, sp, A_Q),
                _mla_prompt(qn, qpe, ckv, kpe, mla_w_uk[e], mla_w_uv[e]).reshape(bp, sp, B_HEADS * MLA_D_V)], -1)
            mix_p = heads @ w_out_even[e]
            moba_k_p.append(ka)
            moba_v_p.append(va)
            mla_p.append(jnp.concatenate([ckv, kpe], -1))
            qa, ka, va, qn, qpe, ckv, kpe = _even_project(xs, pos_s, *pw)
            heads = jnp.concatenate([
                _moba_sample(qa, ka, va, cache_moba_k, cache_moba_v, e, page_table).reshape(bs, ss, A_Q),
                _mla_sample(qn, qpe, ckv, kpe, cache_mla, e, page_table, mla_w_uk[e], mla_w_uv[e]).reshape(bs, ss, B_HEADS * MLA_D_V)], -1)
            mix_s = heads @ w_out_even[e]
            moba_k_s.append(ka)
            moba_v_s.append(va)
            mla_s.append(jnp.concatenate([ckv, kpe], -1))
        else:
            o = l // 2
            groups_p = _odd_project(xp, pos_p, w_in_odd[o])
            groups_s = _odd_project(xs, pos_s, w_in_odd[o])
            outs_p, lses_p, outs_s, lses_s = [], [], [], []
            for g, (win, dil) in enumerate(SWA_CONFIGS):
                qg, kg, vg = groups_p[g]
                og, lg = _dilated_prompt(qg, kg, vg, win, dil)
                outs_p.append(og)
                lses_p.append(lg)
                keep = min(win, sp)
                swa_p[g].append(jnp.stack([kg[:, sp - keep:], vg[:, sp - keep:]], 2))
                qg, kg, vg = groups_s[g]
                og, lg = _dilated_sample(qg, kg, vg, swa_caches[g][o], win, dil)
                outs_s.append(og)
                lses_s.append(lg)
                swa_s[g].append(jnp.stack([kg, vg], 2))
            mix_p = _combine_dilations(outs_p, lses_p).reshape(bp, sp, ODD_OUT) @ w_out_odd[o]
            mix_s = _combine_dilations(outs_s, lses_s).reshape(bs, ss, ODD_OUT) @ w_out_odd[o]
        xp = _layer_norm(DEEPNORM_ALPHA * xp + mix_p, ln_gain[l, 0], ln_bias[l, 0])
        xs = _layer_norm(DEEPNORM_ALPHA * xs + mix_s, ln_gain[l, 0], ln_bias[l, 0])
        mkv = _mem_kv(mem_prompt, w_mem_kv[l])
        mem_p.append(mkv)
        xp = _layer_norm(DEEPNORM_ALPHA * xp + _mem_attn(xp, mkv, w_mem_q[l], w_mem_o[l]), ln_gain[l, 1], ln_bias[l, 1])
        xs = _layer_norm(DEEPNORM_ALPHA * xs + _mem_attn(xs, cache_mem_kv[l], w_mem_q[l], w_mem_o[l]), ln_gain[l, 1], ln_bias[l, 1])
        moe_w = (w_router[l], router_bias[l], w_exp_gate[l], w_exp_up[l], w_exp_down[l],
                 w_sh_gate[l], w_sh_up[l], w_sh_down[l])
        ffn_p = lax.map(lambda xb: _moe(xb, *moe_w), xp.reshape(-1, Q_BLOCK, D_MODEL)).reshape(xp.shape)
        ffn_s = _moe(xs.reshape(-1, D_MODEL), *moe_w).reshape(xs.shape)
        xp = _layer_norm(DEEPNORM_ALPHA * xp + ffn_p, ln_gain[l, 2], ln_bias[l, 2])
        xs = _layer_norm(DEEPNORM_ALPHA * xs + ffn_s, ln_gain[l, 2], ln_bias[l, 2])
    return (xp, xs,
            jnp.stack(moba_k_p), jnp.stack(moba_v_p), jnp.stack(mla_p),
            jnp.stack(swa_p[0]), jnp.stack(swa_p[1]), jnp.stack(swa_p[2]), jnp.stack(mem_p),
            jnp.stack(moba_k_s), jnp.stack(moba_v_s), jnp.stack(mla_s),
            jnp.stack(swa_s[0]), jnp.stack(swa_s[1]), jnp.stack(swa_s[2]))
```

```python
import functools

import numpy as np
import jax
import jax.numpy as jnp
from jax import lax
from jax.experimental import pallas as pl
from jax.experimental.pallas import tpu as pltpu

F32 = jnp.float32
BF16 = jnp.bfloat16
HIGHEST = lax.Precision.HIGHEST

D_MODEL = 1024
DEPTH = 4
PAST_LEN = 8192
PAGE_SIZE = 128
A_HEADS = 8
A_KV_HEADS = 4
A_HEAD_DIM = 64
MOBA_BLOCK = 256
MOBA_TOPK = 3
B_HEADS = 8
MLA_Q_RANK = 384
MLA_KV_RANK = 256
MLA_D_NOPE = 64
MLA_D_ROPE = 32
MLA_D_V = 64
MLA_ROW = MLA_KV_RANK + MLA_D_ROPE
SWA_CONFIGS = ((128, 1), (512, 4), (2048, 16))
N_SWA = 3
C_HEADS = 4
C_HEAD_DIM = 128
SWA_KEYS = 128
N_MEM = 256
MEM_HEADS = 4
MEM_HEAD_DIM = 128
N_EXPERTS = 64
N_EXPERT_GROUPS = 8
TOPK_GROUPS = 4
MOE_TOPK = 8
D_EXPERT = 256
ROUTED_SCALE = 2.5
ROPE_THETA = 10000.0
LN_EPS = 1e-5
RMS_EPS = 1e-6
NEG_INF = -1e30
ALPHA = (2 * DEPTH) ** 0.25
A_SCALE = A_HEAD_DIM ** -0.5
MLA_SCALE = (MLA_D_NOPE + MLA_D_ROPE) ** -0.5
C_SCALE = C_HEAD_DIM ** -0.5
MEM_SCALE = MEM_HEAD_DIM ** -0.5

LANES = 128
EVEN_W = 1792
ODD_W = N_SWA * 3 * C_HEADS * C_HEAD_DIM
VMEM_LIMIT = 56 * 1024 * 1024

_NT = (((1,), (1,)), ((), ()))


def _params(sem, vmem=VMEM_LIMIT):
    return pltpu.CompilerParams(dimension_semantics=sem, vmem_limit_bytes=vmem)


def _dot(a, b):
    return jnp.dot(a, b, preferred_element_type=F32)


def _dot_nt(a, b, precision=None):
    return lax.dot_general(a, b, _NT, precision=precision, preferred_element_type=F32)


def _layer_norm(z, g, b):
    mu = jnp.mean(z, axis=-1, keepdims=True)
    zc = z - mu
    var = jnp.mean(zc * zc, axis=-1, keepdims=True)
    return zc * lax.rsqrt(var + LN_EPS) * g + b


def _rms(x, g):
    return x * lax.rsqrt(jnp.mean(x * x, axis=-1, keepdims=True) + RMS_EPS) * g


def _sigmoid(x):
    return 1.0 / (1.0 + jnp.exp(-x))


def _top_k_mask(vals, k, axis):
    n = vals.shape[axis]
    idx = lax.broadcasted_iota(jnp.int32, vals.shape, axis).astype(F32)
    sel = jnp.zeros(vals.shape, F32)
    for _ in range(k):
        mx = jnp.max(vals, axis=axis, keepdims=True)
        first = jnp.min(jnp.where(vals == mx, idx, float(n)), axis=axis, keepdims=True)
        hit = idx == first
        sel = jnp.where(hit, 1.0, sel)
        vals = jnp.where(hit, -jnp.inf, vals)
    return sel


def _rope_block(yb, cos, sin, half, lane):
    fwd = pltpu.roll(yb, LANES - half, 1)
    bwd = pltpu.roll(yb, half, 1)
    first = (lane & (2 * half - 1)) < half
    return yb * cos + jnp.where(first, fwd, bwd) * sin


def _mm_kernel(x_ref, w_ref, o_ref):
    o_ref[...] = _dot(x_ref[...].astype(BF16), w_ref[...]).astype(o_ref.dtype)


def _matmul(x, w, tm, out_dtype=F32):
    m, k = x.shape
    n = w.shape[1]
    return pl.pallas_call(
        _mm_kernel,
        out_shape=jax.ShapeDtypeStruct((m, n), out_dtype),
        grid=(m // tm,),
        in_specs=[pl.BlockSpec((tm, k), lambda i: (i, 0)),
                  pl.BlockSpec((k, n), lambda i: (0, 0))],
        out_specs=pl.BlockSpec((tm, n), lambda i: (i, 0)),
        compiler_params=_params(("parallel",)),
        name="matmul",
    )(x, w)


def _mm_res_ln_kernel(n_in, *refs):
    a_refs = refs[:n_in]
    w_refs = refs[n_in:2 * n_in]
    r_ref, g_ref, b_ref, o_ref = refs[2 * n_in:]
    acc = None
    for a_ref, w_ref in zip(a_refs, w_refs):
        t = _dot(a_ref[...].astype(BF16), w_ref[...])
        acc = t if acc is None else acc + t
    o_ref[...] = _layer_norm(ALPHA * r_ref[...] + acc, g_ref[...], b_ref[...])


def _mm_res_ln(a_list, w_list, resid, g, b, tm):
    m = resid.shape[0]
    n_in = len(a_list)
    in_specs = [pl.BlockSpec((tm, a.shape[1]), lambda i: (i, 0)) for a in a_list]
    in_specs += [pl.BlockSpec(w.shape, lambda i: (0, 0)) for w in w_list]
    in_specs += [pl.BlockSpec((tm, D_MODEL), lambda i: (i, 0)),
                 pl.BlockSpec((1, D_MODEL), lambda i: (0, 0)),
                 pl.BlockSpec((1, D_MODEL), lambda i: (0, 0))]
    return pl.pallas_call(
        functools.partial(_mm_res_ln_kernel, n_in),
        out_shape=jax.ShapeDtypeStruct((m, D_MODEL), F32),
        grid=(m // tm,),
        in_specs=in_specs,
        out_specs=pl.BlockSpec((tm, D_MODEL), lambda i: (i, 0)),
        compiler_params=_params(("parallel",)),
        name="mm_res_ln",
    )(*a_list, *w_list, resid, g, b)


def _even_proj_kernel(x_ref, tab_ref, win_ref, qn_ref, wuq_ref, kvn_ref, wkv_ref,
                      qa_ref, ka_ref, va_ref, km_ref, mla_ref, qm_ref, kx_ref, vm_ref):
    tm = x_ref.shape[0]
    lane = lax.broadcasted_iota(jnp.int32, (tm, LANES), 1)
    y = _dot(x_ref[...].astype(BF16), win_ref[...])
    cos_a, sin_a = tab_ref[:, 0:128], tab_ref[:, 128:256]
    cos_b, sin_b = tab_ref[:, 256:384], tab_ref[:, 384:512]
    cos_c, sin_c = tab_ref[:, 512:640], tab_ref[:, 640:768]
    for j in range(4):
        qa_ref[:, j * 128:(j + 1) * 128] = _rope_block(
            y[:, j * 128:(j + 1) * 128], cos_a, sin_a, 32, lane)
    for j in range(2):
        kb = _rope_block(y[:, 512 + j * 128:640 + j * 128], cos_a, sin_a, 32, lane)
        ka_ref[:, j * 128:(j + 1) * 128] = kb
        km_ref[0, :, j * 128:(j + 1) * 128] = jnp.mean(kb, axis=0, keepdims=True)
    va_ref[...] = y[:, 768:1024]
    cq = _rms(y[:, 1024:1408], qn_ref[...])
    qm = _dot(cq.astype(BF16), wuq_ref[...])
    for j in range(B_HEADS):
        qj = _rope_block(qm[:, j * 128:(j + 1) * 128], cos_b, sin_b, 16, lane)
        qm_ref[:, j * 128:(j + 1) * 128] = (qj * MLA_SCALE).astype(BF16)
    ckv = _rms(y[:, 1408:1664], kvn_ref[...])
    kpe = _rope_block(y[:, 1664:1792], cos_c, sin_c, 16, lane)
    mla_ref[:, 0:MLA_KV_RANK] = ckv
    mla_ref[:, MLA_KV_RANK:MLA_ROW] = kpe[:, 0:MLA_D_ROPE]
    ck = jnp.concatenate([ckv, kpe], axis=1).astype(BF16)
    kv = _dot(ck, wkv_ref[...])
    kx_ref[...] = kv[:, 0:1024].astype(BF16)
    vm_ref[...] = kv[:, 1024:1536].astype(BF16)


def _even_proj(x, tab, tab_blocks, w_in, q_norm, w_uq, kv_norm, w_kv, tm):
    m = x.shape[0]
    nt = m // tm
    row = lambda i: (i, 0)
    const = lambda i: (0, 0)
    outs = (
        jax.ShapeDtypeStruct((m, 512), F32),
        jax.ShapeDtypeStruct((m, 256), F32),
        jax.ShapeDtypeStruct((m, 256), F32),
        jax.ShapeDtypeStruct((nt, 1, 256), F32),
        jax.ShapeDtypeStruct((m, MLA_ROW), F32),
        jax.ShapeDtypeStruct((m, 1024), BF16),
        jax.ShapeDtypeStruct((m, 1024), BF16),
        jax.ShapeDtypeStruct((m, 512), BF16),
    )
    return pl.pallas_call(
        _even_proj_kernel,
        out_shape=outs,
        grid=(nt,),
        in_specs=[pl.BlockSpec((tm, D_MODEL), row),
                  pl.BlockSpec((tm, 768), lambda i: (i % tab_blocks, 0)),
                  pl.BlockSpec(w_in.shape, const),
                  pl.BlockSpec(q_norm.shape, const),
                  pl.BlockSpec(w_uq.shape, const),
                  pl.BlockSpec(kv_norm.shape, const),
                  pl.BlockSpec(w_kv.shape, const)],
        out_specs=(pl.BlockSpec((tm, 512), row),
                   pl.BlockSpec((tm, 256), row),
                   pl.BlockSpec((tm, 256), row),
                   pl.BlockSpec((1, 1, 256), lambda i: (i, 0, 0)),
                   pl.BlockSpec((tm, MLA_ROW), row),
                   pl.BlockSpec((tm, 1024), row),
                   pl.BlockSpec((tm, 1024), row),
                   pl.BlockSpec((tm, 512), row)),
        compiler_params=_params(("parallel",)),
        name="even_proj",
    )(x, tab, w_in, q_norm, w_uq, kv_norm, w_kv)


def _moba_p_kernel(q_ref, k_ref, v_ref, km_ref, o_ref, m_s, l_s, acc_s):
    tq = MOBA_BLOCK
    c = pl.program_id(2)
    n_blk = km_ref.shape[0]
    lane = lax.broadcasted_iota(jnp.int32, (tq, LANES), 1)
    lo = lane < A_HEAD_DIM
    q_a = q_ref[:, 0:128]
    q_b = q_ref[:, 128:256]
    q4 = jnp.concatenate([jnp.where(lo, q_a, 0.0), jnp.where(lo, 0.0, q_a),
                          jnp.where(lo, q_b, 0.0), jnp.where(lo, 0.0, q_b)], axis=0)
    gate = _dot_nt(q4, km_ref[...], precision=HIGHEST)
    blk = lax.broadcasted_iota(jnp.int32, gate.shape, 1)
    past = blk < c
    sel = _top_k_mask(jnp.where(past, gate, NEG_INF), min(MOBA_TOPK, n_blk), 1)
    sel = jnp.where(past, sel, 0.0)

    qb = (q4 * A_SCALE).astype(BF16)

    def kv_tile(j):
        start = pl.multiple_of(j * tq, tq)
        return (k_ref[pl.ds(start, tq), :].astype(BF16), v_ref[pl.ds(start, tq), :].astype(BF16))

    k_own, v_own = kv_tile(c)
    s = _dot_nt(qb, k_own)
    row = lax.broadcasted_iota(jnp.int32, s.shape, 0) & (tq - 1)
    col = lax.broadcasted_iota(jnp.int32, s.shape, 1)
    s = jnp.where(col <= row, s, NEG_INF)
    m0 = jnp.max(s, axis=1, keepdims=True)
    p = jnp.exp(s - m0)
    m_s[...] = m0
    l_s[...] = jnp.sum(p, axis=1, keepdims=True)
    acc_s[...] = _dot(p.astype(BF16), v_own)

    def body(j, carry):
        kj, vj = kv_tile(j)
        sj = _dot_nt(qb, kj)
        chosen = jnp.sum(jnp.where(blk == j, sel, 0.0), axis=1, keepdims=True) > 0.5
        sj = jnp.where(chosen, sj, NEG_INF)
        m_old = m_s[...]
        m_new = jnp.maximum(m_old, jnp.max(sj, axis=1, keepdims=True))
        a = jnp.exp(m_old - m_new)
        pj = jnp.exp(sj - m_new)
        m_s[...] = m_new
        l_s[...] = a * l_s[...] + jnp.sum(pj, axis=1, keepdims=True)
        acc_s[...] = a * acc_s[...] + _dot(pj.astype(BF16), vj)
        return carry

    lax.fori_loop(0, c, body, 0)
    o = acc_s[...] / l_s[...]
    o_ref[:, 0:128] = jnp.where(lo, o[0:tq], o[tq:2 * tq])
    o_ref[:, 128:256] = jnp.where(lo, o[2 * tq:3 * tq], o[3 * tq:4 * tq])


def _moba_prompt(qa, ka, va, kmean, bsz, seq):
    tq = MOBA_BLOCK
    nc = seq // tq
    return pl.pallas_call(
        _moba_p_kernel,
        out_shape=jax.ShapeDtypeStruct((bsz * seq, 512), F32),
        grid=(bsz, 2, nc),
        in_specs=[pl.BlockSpec((tq, 256), lambda b, p, c: (b * nc + c, p)),
                  pl.BlockSpec((seq, 128), lambda b, p, c: (b, p)),
                  pl.BlockSpec((seq, 128), lambda b, p, c: (b, p)),
                  pl.BlockSpec((nc, 128), lambda b, p, c: (b, p))],
        out_specs=pl.BlockSpec((tq, 256), lambda b, p, c: (b * nc + c, p)),
        scratch_shapes=[pltpu.VMEM((4 * tq, 1), F32), pltpu.VMEM((4 * tq, 1), F32),
                        pltpu.VMEM((4 * tq, 128), F32)],
        compiler_params=_params(("parallel", "parallel", "arbitrary")),
        name="moba_prompt",
    )(qa, ka, va, kmean)


def _mla_p_kernel(q_ref, k_ref, v_ref, o_ref, m_s, l_s, acc_s):
    tq = q_ref.shape[0]
    c = pl.program_id(2)
    lane = lax.broadcasted_iota(jnp.int32, (tq, LANES), 1)
    lo = lane < MLA_D_V
    q0 = q_ref[:, 0:128]
    q1 = q_ref[:, 128:256]

    def scores(j):
        start = pl.multiple_of(j * tq, tq)
        k = k_ref[pl.ds(start, tq), :]
        v = v_ref[pl.ds(start, tq), :]
        s = jnp.concatenate([_dot_nt(q0, k[:, 0:128]), _dot_nt(q1, k[:, 128:256])], axis=0)
        return s, v

    s, v = scores(c)
    row = lax.broadcasted_iota(jnp.int32, s.shape, 0) & (tq - 1)
    col = lax.broadcasted_iota(jnp.int32, s.shape, 1)
    s = jnp.where(col <= row, s, NEG_INF)
    m0 = jnp.max(s, axis=1, keepdims=True)
    p = jnp.exp(s - m0)
    m_s[...] = m0
    l_s[...] = jnp.sum(p, axis=1, keepdims=True)
    acc_s[...] = _dot(p.astype(BF16), v)

    def body(j, carry):
        sj, vj = scores(j)
        m_old = m_s[...]
        m_new = jnp.maximum(m_old, jnp.max(sj, axis=1, keepdims=True))
        a = jnp.exp(m_old - m_new)
        pj = jnp.exp(sj - m_new)
        m_s[...] = m_new
        l_s[...] = a * l_s[...] + jnp.sum(pj, axis=1, keepdims=True)
        acc_s[...] = a * acc_s[...] + _dot(pj.astype(BF16), vj)
        return carry

    lax.fori_loop(0, c, body, 0)
    o = acc_s[...] / l_s[...]
    o_ref[...] = jnp.where(lo, o[0:tq], o[tq:2 * tq])


def _mla_prompt(qm, kx, vm, bsz, seq, tq=256):
    nc = seq // tq
    return pl.pallas_call(
        _mla_p_kernel,
        out_shape=jax.ShapeDtypeStruct((bsz * seq, 512), F32),
        grid=(bsz, B_HEADS // 2, nc),
        in_specs=[pl.BlockSpec((tq, 256), lambda b, h, c: (b * nc + c, h)),
                  pl.BlockSpec((seq, 256), lambda b, h, c: (b, h)),
                  pl.BlockSpec((seq, 128), lambda b, h, c: (b, h))],
        out_specs=pl.BlockSpec((tq, 128), lambda b, h, c: (b * nc + c, h)),
        scratch_shapes=[pltpu.VMEM((2 * tq, 1), F32), pltpu.VMEM((2 * tq, 1), F32),
                        pltpu.VMEM((2 * tq, 128), F32)],
        compiler_params=_params(("parallel", "parallel", "arbitrary")),
        name="mla_prompt",
    )(qm, kx, vm)


def _page_copies(pt_ref, cache_hbm, buf, sem, bb, slot, n_pages, page_base):
    copies = []
    for j in range(n_pages):
        pg = pt_ref[bb, j] + page_base
        copies.append(pltpu.make_async_copy(
            cache_hbm.at[pg], buf.at[slot, pl.ds(j * PAGE_SIZE, PAGE_SIZE)], sem.at[slot]))
    return copies


def _prefetch_pages(pt_ref, caches, bufs, sems, n_pages, page_base):
    b = pl.program_id(0)
    nb = pl.num_programs(0)
    slot = b % 2

    def start(bb, sl):
        for cache, buf, sem in zip(caches, bufs, sems):
            for cp in _page_copies(pt_ref, cache, buf, sem, bb, sl, n_pages, page_base):
                cp.start()

    @pl.when(b == 0)
    def _():
        start(0, 0)

    @pl.when(b + 1 < nb)
    def _():
        start(b + 1, 1 - slot)

    for cache, buf, sem in zip(caches, bufs, sems):
        for cp in _page_copies(pt_ref, cache, buf, sem, b, slot, n_pages, page_base):
            cp.wait()
    return slot


def _moba_s_kernel(page_base, pt_ref, q_ref, kn_ref, vn_ref, kc_hbm, vc_hbm, o_ref,
                   kbuf, vbuf, s_s, km_s, ksem, vsem):
    n_keys = kbuf.shape[1]
    n_pages = n_keys // PAGE_SIZE
    n_blk = n_keys // MOBA_BLOCK
    chunk = 1024
    n_chunk = n_keys // chunk
    rows = q_ref.shape[1]
    slot = _prefetch_pages(pt_ref, (kc_hbm, vc_hbm), (kbuf, vbuf), (ksem, vsem), n_pages, page_base)

    q = q_ref[0]
    qb = (q * A_SCALE).astype(BF16)
    for n in range(n_blk):
        km_s[n:n + 1, :] = jnp.mean(kbuf[slot, n * MOBA_BLOCK:(n + 1) * MOBA_BLOCK, :], axis=0, keepdims=True)
    for ci in range(n_chunk):
        kc = kbuf[slot, ci * chunk:(ci + 1) * chunk, :]
        s_s[:, ci * chunk:(ci + 1) * chunk] = _dot_nt(qb, kc.astype(BF16))
    gate = _dot_nt(q, km_s[...], precision=HIGHEST)
    sel = _top_k_mask(gate, min(MOBA_TOPK, n_blk), 1)
    key_blk = lax.broadcasted_iota(jnp.int32, (n_blk, n_keys), 1) >> 8
    expand = jnp.where(key_blk == lax.broadcasted_iota(jnp.int32, (n_blk, n_keys), 0), 1.0, 0.0)
    chosen = _dot(sel.astype(BF16), expand.astype(BF16)) > 0.5
    s = jnp.where(chosen, s_s[...], NEG_INF)

    t_new = kn_ref.shape[1]
    s_new = _dot_nt(qb, kn_ref[0].astype(BF16))
    tok = (lax.broadcasted_iota(jnp.int32, s_new.shape, 0) & 7) >> 1
    s_new = jnp.where(lax.broadcasted_iota(jnp.int32, s_new.shape, 1) <= tok, s_new, NEG_INF)
    m = jnp.maximum(jnp.max(s, axis=1, keepdims=True), jnp.max(s_new, axis=1, keepdims=True))
    p = jnp.exp(s - m)
    p_new = jnp.exp(s_new - m)
    l = jnp.sum(p, axis=1, keepdims=True) + jnp.sum(p_new, axis=1, keepdims=True)
    acc = _dot(p_new.astype(BF16), vn_ref[0].astype(BF16))
    pb = p.astype(BF16)
    for ci in range(n_chunk):
        vc = vbuf[slot, ci * chunk:(ci + 1) * chunk, :].astype(BF16)
        acc = acc + _dot(pb[:, ci * chunk:(ci + 1) * chunk], vc)
    o = acc / l
    lane_head = lax.broadcasted_iota(jnp.int32, (8, o.shape[1]), 1) >> 6
    out = jnp.zeros((8, o.shape[1]), F32)
    for k in range(A_KV_HEADS):
        out = out + jnp.where(lane_head == k, o[8 * k:8 * k + 8], 0.0)
    o_ref[0] = out


def _moba_sample(page_table, q_rows, k_new, v_new, cache_k, cache_v, page_base):
    dbs, n_pages = page_table.shape
    n_keys = n_pages * PAGE_SIZE
    grid_spec = pltpu.PrefetchScalarGridSpec(
        num_scalar_prefetch=1,
        grid=(dbs,),
        in_specs=[pl.BlockSpec((1,) + q_rows.shape[1:], lambda b, pt: (b, 0, 0)),
                  pl.BlockSpec((1,) + k_new.shape[1:], lambda b, pt: (b, 0, 0)),
                  pl.BlockSpec((1,) + v_new.shape[1:], lambda b, pt: (b, 0, 0)),
                  pl.BlockSpec(memory_space=pl.ANY),
                  pl.BlockSpec(memory_space=pl.ANY)],
        out_specs=pl.BlockSpec((1, 8, 256), lambda b, pt: (b, 0, 0)),
        scratch_shapes=[pltpu.VMEM((2, n_keys, 256), F32), pltpu.VMEM((2, n_keys, 256), F32),
                        pltpu.VMEM((q_rows.shape[1], n_keys), F32),
                        pltpu.VMEM((n_keys // MOBA_BLOCK, 256), F32),
                        pltpu.SemaphoreType.DMA((2,)), pltpu.SemaphoreType.DMA((2,))],
    )
    return pl.pallas_call(
        functools.partial(_moba_s_kernel, page_base),
        out_shape=jax.ShapeDtypeStruct((dbs, 8, 256), F32),
        grid_spec=grid_spec,
        compiler_params=_params(("arbitrary",), 60 * 1024 * 1024),
        name="moba_sample",
    )(page_table, q_rows, k_new, v_new, cache_k, cache_v)


def _mla_s_kernel(page_base, pt_ref, q_ref, new_ref, c_hbm, o_ref, cbuf, s_s, csem):
    n_keys = cbuf.shape[1]
    n_pages = n_keys // PAGE_SIZE
    chunk = 1024
    n_chunk = n_keys // chunk
    slot = _prefetch_pages(pt_ref, (c_hbm,), (cbuf,), (csem,), n_pages, page_base)

    q = q_ref[0][:, 0:MLA_ROW]
    for ci in range(n_chunk):
        pc = cbuf[slot, ci * chunk:(ci + 1) * chunk, :].astype(BF16)
        s_s[:, ci * chunk:(ci + 1) * chunk] = _dot_nt(q, pc)
    s = s_s[...]
    new = new_ref[0]
    s_new = _dot_nt(q, new.astype(BF16))
    tok = lax.broadcasted_iota(jnp.int32, s_new.shape, 0) >> 3
    s_new = jnp.where(lax.broadcasted_iota(jnp.int32, s_new.shape, 1) <= tok, s_new, NEG_INF)
    m = jnp.maximum(jnp.max(s, axis=1, keepdims=True), jnp.max(s_new, axis=1, keepdims=True))
    p = jnp.exp(s - m)
    p_new = jnp.exp(s_new - m)
    l = jnp.sum(p, axis=1, keepdims=True) + jnp.sum(p_new, axis=1, keepdims=True)
    acc = _dot(p_new.astype(BF16), new[:, 0:MLA_KV_RANK].astype(BF16))
    pb = p.astype(BF16)
    for ci in range(n_chunk):
        cc = cbuf[slot, ci * chunk:(ci + 1) * chunk, 0:MLA_KV_RANK].astype(BF16)
        acc = acc + _dot(pb[:, ci * chunk:(ci + 1) * chunk], cc)
    o_ref[0] = acc / l


def _mla_sample(page_table, q_rows, new_rows, cache, page_base):
    dbs, n_pages = page_table.shape
    n_keys = n_pages * PAGE_SIZE
    grid_spec = pltpu.PrefetchScalarGridSpec(
        num_scalar_prefetch=1,
        grid=(dbs,),
        in_specs=[pl.BlockSpec((1,) + q_rows.shape[1:], lambda b, pt: (b, 0, 0)),
                  pl.BlockSpec((1,) + new_rows.shape[1:], lambda b, pt: (b, 0, 0)),
                  pl.BlockSpec(memory_space=pl.ANY)],
        out_specs=pl.BlockSpec((1, q_rows.shape[1], MLA_KV_RANK), lambda b, pt: (b, 0, 0)),
        scratch_shapes=[pltpu.VMEM((2, n_keys, MLA_ROW), F32),
                        pltpu.VMEM((q_rows.shape[1], n_keys), F32),
                        pltpu.SemaphoreType.DMA((2,))],
    )
    return pl.pallas_call(
        functools.partial(_mla_s_kernel, page_base),
        out_shape=jax.ShapeDtypeStruct((dbs, q_rows.shape[1], MLA_KV_RANK), F32),
        grid_spec=grid_spec,
        compiler_params=_params(("arbitrary",), 60 * 1024 * 1024),
        name="mla_sample",
    )(page_table, q_rows, new_rows, cache)


def _odd_proj_kernel(x_ref, tab_ref, w_ref, o_ref):
    tm = x_ref.shape[0]
    lane = lax.broadcasted_iota(jnp.int32, (tm, LANES), 1)
    cos, sin = tab_ref[:, 0:128], tab_ref[:, 128:256]
    xb = x_ref[...].astype(BF16)
    for g in range(N_SWA):
        base = g * 1536
        y = _dot(xb, w_ref[:, base:base + 1536])
        for j in range(8):
            o_ref[:, base + j * 128:base + (j + 1) * 128] = _rope_block(
                y[:, j * 128:(j + 1) * 128], cos, sin, 64, lane)
        o_ref[:, base + 1024:base + 1536] = y[:, 1024:1536]


def _odd_proj(x, tab, tab_blocks, w, tm):
    m = x.shape[0]
    return pl.pallas_call(
        _odd_proj_kernel,
        out_shape=jax.ShapeDtypeStruct((m, ODD_W), F32),
        grid=(m // tm,),
        in_specs=[pl.BlockSpec((tm, D_MODEL), lambda i: (i, 0)),
                  pl.BlockSpec((tm, 256), lambda i: (i % tab_blocks, 0)),
                  pl.BlockSpec(w.shape, lambda i: (0, 0))],
        out_specs=pl.BlockSpec((tm, ODD_W), lambda i: (i, 0)),
        compiler_params=_params(("parallel",)),
        name="odd_proj",
    )(x, tab, w)


def _dil_p_kernel(q_ref, kp_ref, kc_ref, vp_ref, vc_ref, o_ref, lse_ref):
    tq = SWA_KEYS
    mt = pl.program_id(2)
    i = lax.broadcasted_iota(jnp.int32, (tq, 2 * tq), 0)
    kk = lax.broadcasted_iota(jnp.int32, (tq, 2 * tq), 1)
    first_ok = jnp.where(mt > 0, 0, tq)
    band = (kk >= i) & (kk <= i + tq) & (kk >= first_ok)
    lane = lax.broadcasted_iota(jnp.int32, (tq, LANES), 1)
    stats = jnp.zeros((tq, LANES), F32)
    for h in range(C_HEADS):
        hs = slice(h * 128, (h + 1) * 128)
        q = (q_ref[0, :, hs] * C_SCALE).astype(BF16)
        k = jnp.concatenate([kp_ref[0, :, hs], kc_ref[0, :, hs]], axis=0).astype(BF16)
        v = jnp.concatenate([vp_ref[0, :, hs], vc_ref[0, :, hs]], axis=0).astype(BF16)
        s = jnp.where(band, _dot_nt(q, k), NEG_INF)
        m = jnp.max(s, axis=1, keepdims=True)
        p = jnp.exp(s - m)
        l = jnp.sum(p, axis=1, keepdims=True)
        o_ref[0, :, hs] = _dot(p.astype(BF16), v) / l
        stats = jnp.where(lane == h, m + jnp.log(l), stats)
    lse_ref[0] = stats


def _dilated_prompt(h3, g, dil):
    bsz, seq, _ = h3.shape
    sd = seq // dil
    tq = SWA_KEYS
    nt = sd // tq
    hv = h3.reshape(bsz, sd, dil * ODD_W)
    blocks_per_pos = ODD_W // 512
    qi, ki, vi = 3 * g, 3 * g + 1, 3 * g + 2
    prev = lambda mt: jnp.maximum(mt - 1, 0)
    o, lse = pl.pallas_call(
        _dil_p_kernel,
        out_shape=(jax.ShapeDtypeStruct((bsz, sd, dil * 512), F32),
                   jax.ShapeDtypeStruct((bsz, sd, dil * 128), F32)),
        grid=(bsz, dil, nt),
        in_specs=[pl.BlockSpec((1, tq, 512), lambda b, r, mt: (b, mt, r * blocks_per_pos + qi)),
                  pl.BlockSpec((1, tq, 512), lambda b, r, mt: (b, prev(mt), r * blocks_per_pos + ki)),
                  pl.BlockSpec((1, tq, 512), lambda b, r, mt: (b, mt, r * blocks_per_pos + ki)),
                  pl.BlockSpec((1, tq, 512), lambda b, r, mt: (b, prev(mt), r * blocks_per_pos + vi)),
                  pl.BlockSpec((1, tq, 512), lambda b, r, mt: (b, mt, r * blocks_per_pos + vi))],
        out_specs=(pl.BlockSpec((1, tq, 512), lambda b, r, mt: (b, mt, r)),
                   pl.BlockSpec((1, tq, 128), lambda b, r, mt: (b, mt, r))),
        compiler_params=_params(("parallel", "parallel", "arbitrary")),
        name="dilated_prompt",
    )(hv, hv, hv, hv, hv)
    return o.reshape(bsz * seq, 512), lse.reshape(bsz * seq, 128)


def _dil_s_kernel(h_ref, c0_ref, c1_ref, c2_ref, o0_ref, o1_ref, o2_ref, l0_ref, l1_ref, l2_ref):
    t_new = h_ref.shape[1]
    caches = (c0_ref, c1_ref, c2_ref)
    o_refs = (o0_ref, o1_ref, o2_ref)
    l_refs = (l0_ref, l1_ref, l2_ref)
    key_i = lax.broadcasted_iota(jnp.int32, (SWA_KEYS, 1), 0)
    new_i = lax.broadcasted_iota(jnp.int32, (t_new, 1), 0)
    st_row = lax.broadcasted_iota(jnp.int32, (t_new, LANES), 0)
    st_lane = lax.broadcasted_iota(jnp.int32, (t_new, LANES), 1)
    for g in range(N_SWA):
        base = g * 1536
        stats = jnp.zeros((t_new, LANES), F32)
        for t in range(t_new):
            for h in range(C_HEADS):
                q = h_ref[0, t:t + 1, base + h * 128:base + (h + 1) * 128] * C_SCALE
                k_new = h_ref[0, :, base + 512 + h * 128:base + 512 + (h + 1) * 128]
                v_new = h_ref[0, :, base + 1024 + h * 128:base + 1024 + (h + 1) * 128]
                if g == 0:
                    col = h * 128
                    old_ok = key_i >= t
                    new_ok = new_i <= t
                else:
                    col = t * 1024 + h * 128
                    old_ok = None
                    new_ok = new_i == t
                k_old = caches[g][0, :, col:col + 128]
                v_old = caches[g][0, :, col + 512:col + 640]
                s_old = jnp.sum(k_old * q, axis=1, keepdims=True)
                if old_ok is not None:
                    s_old = jnp.where(old_ok, s_old, NEG_INF)
                s_new = jnp.where(new_ok, jnp.sum(k_new * q, axis=1, keepdims=True), NEG_INF)
                m = jnp.maximum(jnp.max(s_old, axis=0, keepdims=True), jnp.max(s_new, axis=0, keepdims=True))
                p_old = jnp.exp(s_old - m)
                p_new = jnp.exp(s_new - m)
                l = jnp.sum(p_old, axis=0, keepdims=True) + jnp.sum(p_new, axis=0, keepdims=True)
                o = (jnp.sum(p_old * v_old, axis=0, keepdims=True)
                     + jnp.sum(p_new * v_new, axis=0, keepdims=True)) / l
                o_refs[g][0, t:t + 1, h * 128:(h + 1) * 128] = o
                stats = jnp.where((st_row == t) & (st_lane == h), m + jnp.log(l), stats)
        l_refs[g][0] = stats


def _dilated_sample(h_s, c0, c1, c2, row_base):
    dbs, t_new, _ = h_s.shape
    o_sds = jax.ShapeDtypeStruct((dbs, t_new, 512), F32)
    l_sds = jax.ShapeDtypeStruct((dbs, t_new, 128), F32)
    idx = lambda b: (b, 0, 0)
    cidx = lambda b: (row_base + b, 0, 0)
    outs = pl.pallas_call(
        _dil_s_kernel,
        out_shape=(o_sds,) * 3 + (l_sds,) * 3,
        grid=(dbs,),
        in_specs=[pl.BlockSpec((1, t_new, ODD_W), idx),
                  pl.BlockSpec((1, SWA_KEYS, 1024), cidx),
                  pl.BlockSpec((1, SWA_KEYS, t_new * 1024), cidx),
                  pl.BlockSpec((1, SWA_KEYS, t_new * 1024), cidx)],
        out_specs=(pl.BlockSpec((1, t_new, 512), idx),) * 3 + (pl.BlockSpec((1, t_new, 128), idx),) * 3,
        compiler_params=_params(("parallel",)),
        name="dilated_sample",
    )(h_s, c0, c1, c2)
    return outs[:3], outs[3:]


def _odd_out_kernel(o0_ref, o1_ref, o2_ref, l0_ref, l1_ref, l2_ref, w_ref, r_ref, g_ref, b_ref, out_ref):
    parts = []
    for h in range(C_HEADS):
        hs = slice(h * 128, (h + 1) * 128)
        a0, a1, a2 = l0_ref[:, h:h + 1], l1_ref[:, h:h + 1], l2_ref[:, h:h + 1]
        mx = jnp.maximum(jnp.maximum(a0, a1), a2)
        e0, e1, e2 = jnp.exp(a0 - mx), jnp.exp(a1 - mx), jnp.exp(a2 - mx)
        den = e0 + e1 + e2
        parts.append((e0 / den) * o0_ref[:, hs] + (e1 / den) * o1_ref[:, hs] + (e2 / den) * o2_ref[:, hs])
    comb = jnp.concatenate(parts, axis=1).astype(BF16)
    out_ref[...] = _layer_norm(ALPHA * r_ref[...] + _dot(comb, w_ref[...]), g_ref[...], b_ref[...])


def _odd_out(outs, lses, w, resid, g, b, tm):
    m = resid.shape[0]
    row = lambda i: (i, 0)
    const = lambda i: (0, 0)
    return pl.pallas_call(
        _odd_out_kernel,
        out_shape=jax.ShapeDtypeStruct((m, D_MODEL), F32),
        grid=(m // tm,),
        in_specs=[pl.BlockSpec((tm, 512), row)] * 3 + [pl.BlockSpec((tm, 128), row)] * 3
        + [pl.BlockSpec(w.shape, const), pl.BlockSpec((tm, D_MODEL), row),
           pl.BlockSpec((1, D_MODEL), const), pl.BlockSpec((1, D_MODEL), const)],
        out_specs=pl.BlockSpec((tm, D_MODEL), row),
        compiler_params=_params(("parallel",)),
        name="odd_out",
    )(*outs, *lses, w, resid, g, b)


def _mem_heads(q, kv_ref_val):
    outs = []
    for h in range(MEM_HEADS):
        k = kv_ref_val[:, h * 128:(h + 1) * 128]
        v = kv_ref_val[:, 512 + h * 128:512 + (h + 1) * 128]
        s = _dot_nt(q[:, h * 128:(h + 1) * 128].astype(BF16), k)
        m = jnp.max(s, axis=1, keepdims=True)
        p = jnp.exp(s - m)
        l = jnp.sum(p, axis=1, keepdims=True)
        outs.append(_dot(p.astype(BF16), v) / l)
    return jnp.concatenate(outs, axis=1)


def _mem_p_kernel(x_ref, kv_ref, wq_ref, wo_ref, g_ref, b_ref, o_ref):
    x = x_ref[...]
    q = _dot(x.astype(BF16), wq_ref[...]) * MEM_SCALE
    att = _mem_heads(q, kv_ref[0])
    o_ref[...] = _layer_norm(ALPHA * x + _dot(att.astype(BF16), wo_ref[...]), g_ref[...], b_ref[...])


def _mem_prompt(x, kv, wq, wo, g, b, tiles_per_batch, tm):
    m = x.shape[0]
    row = lambda i: (i, 0)
    const = lambda i: (0, 0)
    return pl.pallas_call(
        _mem_p_kernel,
        out_shape=jax.ShapeDtypeStruct((m, D_MODEL), F32),
        grid=(m // tm,),
        in_specs=[pl.BlockSpec((tm, D_MODEL), row),
                  pl.BlockSpec((1, N_MEM, 1024), lambda i: (i // tiles_per_batch, 0, 0)),
                  pl.BlockSpec(wq.shape, const), pl.BlockSpec(wo.shape, const),
                  pl.BlockSpec((1, D_MODEL), const), pl.BlockSpec((1, D_MODEL), const)],
        out_specs=pl.BlockSpec((tm, D_MODEL), row),
        compiler_params=_params(("parallel",)),
        name="mem_prompt",
    )(x, kv, wq, wo, g, b)


def _mem_s_kernel(q_ref, kv_ref, o_ref):
    o_ref[0] = _mem_heads(q_ref[0] * MEM_SCALE, kv_ref[0].astype(BF16))


def _mem_sample_attn(q, kv, row_base):
    dbs, t_new, _ = q.shape
    idx = lambda b: (b, 0, 0)
    return pl.pallas_call(
        _mem_s_kernel,
        out_shape=jax.ShapeDtypeStruct(q.shape, F32),
        grid=(dbs,),
        in_specs=[pl.BlockSpec((1, t_new, 512), idx),
                  pl.BlockSpec((1, N_MEM, 1024), lambda b: (row_base + b, 0, 0))],
        out_specs=pl.BlockSpec((1, t_new, 512), idx),
        compiler_params=_params(("parallel",)),
        name="mem_sample",
    )(q, kv)


def _router_kernel(x_ref, wr_ref, rb_ref, gate_ref):
    tm = x_ref.shape[0]
    per_grp = N_EXPERTS // N_EXPERT_GROUPS
    logits = _dot_nt(wr_ref[...], x_ref[...], precision=HIGHEST)
    s = _sigmoid(logits)
    sb = s + rb_ref[...]
    sub = lax.broadcasted_iota(jnp.int32, (per_grp, tm), 0).astype(F32)
    slabs, gscore = [], []
    for gi in range(N_EXPERT_GROUPS):
        slab = sb[gi * per_grp:(gi + 1) * per_grp, :]
        m1 = jnp.max(slab, axis=0, keepdims=True)
        f1 = jnp.min(jnp.where(slab == m1, sub, float(per_grp)), axis=0, keepdims=True)
        m2 = jnp.max(jnp.where(sub == f1, -jnp.inf, slab), axis=0, keepdims=True)
        slabs.append(slab)
        gscore.append(m1 + m2)
    masked = []
    for gi in range(N_EXPERT_GROUPS):
        beaten = jnp.zeros((1, tm), F32)
        for gj in range(N_EXPERT_GROUPS):
            if gj < gi:
                beaten = beaten + jnp.where(gscore[gj] >= gscore[gi], 1.0, 0.0)
            elif gj > gi:
                beaten = beaten + jnp.where(gscore[gj] > gscore[gi], 1.0, 0.0)
        masked.append(jnp.where(beaten < TOPK_GROUPS - 0.5, slabs[gi], NEG_INF))
    esel = _top_k_mask(jnp.concatenate(masked, axis=0), MOE_TOPK, 0)
    w = s * esel
    w = w / jnp.sum(w, axis=0, keepdims=True) * ROUTED_SCALE
    wpad = jnp.concatenate([w, jnp.zeros((LANES - N_EXPERTS, tm), F32)], axis=0)
    gate_ref[...] = wpad.T


def _router(x, wr_t, rb, tm):
    m = x.shape[0]
    tm = min(tm, m)
    return pl.pallas_call(
        _router_kernel,
        out_shape=jax.ShapeDtypeStruct((m, LANES), F32),
        grid=(m // tm,),
        in_specs=[pl.BlockSpec((tm, D_MODEL), lambda i: (i, 0)),
                  pl.BlockSpec(wr_t.shape, lambda i: (0, 0)),
                  pl.BlockSpec(rb.shape, lambda i: (0, 0))],
        out_specs=pl.BlockSpec((tm, LANES), lambda i: (i, 0)),
        compiler_params=_params(("parallel",)),
        name="router",
    )(x, wr_t, rb)


def _moe_kernel(x_ref, gate_ref, wg_ref, wu_ref, wd_ref, sg_ref, su_ref, sd_ref, g_ref, b_ref,
                o_ref, xb_s):
    e = pl.program_id(1)

    @pl.when(e == 0)
    def _():
        xb0 = x_ref[...].astype(BF16)
        xb_s[...] = xb0
        hg = _dot(xb0, sg_ref[...])
        hs = hg * _sigmoid(hg) * _dot(xb0, su_ref[...])
        o_ref[...] = _dot(hs.astype(BF16), sd_ref[...])

    xb = xb_s[...]
    hg = _dot(xb, wg_ref[0])
    h = hg * _sigmoid(hg) * _dot(xb, wu_ref[0])
    gate = gate_ref[...]
    lane = lax.broadcasted_iota(jnp.int32, gate.shape, 1)
    gcol = jnp.sum(jnp.where(lane == e, gate, 0.0), axis=1, keepdims=True)
    o_ref[...] += _dot((h * gcol).astype(BF16), wd_ref[0])

    @pl.when(e == pl.num_programs(1) - 1)
    def _():
        o_ref[...] = _layer_norm(ALPHA * x_ref[...] + o_ref[...], g_ref[...], b_ref[...])


def _moe(x, gate, wg, wu, wd, sg, su, sd, g, b, tm):
    m = x.shape[0]
    tm = min(tm, m)
    row = lambda i, e: (i, 0)
    const = lambda i, e: (0, 0)
    return pl.pallas_call(
        _moe_kernel,
        out_shape=jax.ShapeDtypeStruct((m, D_MODEL), F32),
        grid=(m // tm, N_EXPERTS),
        in_specs=[pl.BlockSpec((tm, D_MODEL), row),
                  pl.BlockSpec((tm, LANES), row),
                  pl.BlockSpec((1, D_MODEL, D_EXPERT), lambda i, e: (e, 0, 0)),
                  pl.BlockSpec((1, D_MODEL, D_EXPERT), lambda i, e: (e, 0, 0)),
                  pl.BlockSpec((1, D_EXPERT, D_MODEL), lambda i, e: (e, 0, 0)),
                  pl.BlockSpec(sg.shape, const), pl.BlockSpec(su.shape, const),
                  pl.BlockSpec(sd.shape, const),
                  pl.BlockSpec((1, D_MODEL), const), pl.BlockSpec((1, D_MODEL), const)],
        out_specs=pl.BlockSpec((tm, D_MODEL), row),
        scratch_shapes=[pltpu.VMEM((tm, D_MODEL), BF16)],
        compiler_params=_params(("parallel", "arbitrary")),
        name="moe",
    )(x, gate, wg, wu, wd, sg, su, sd, g, b)


def _rope_cs(pos, half):
    inv = ROPE_THETA ** (-jnp.arange(half, dtype=F32) / half)
    ang = pos.astype(F32)[:, None] * inv[None, :]
    return jnp.cos(ang), jnp.sin(ang)


def _even_table(pos):
    n = pos.shape[0]
    c32, s32 = _rope_cs(pos, 32)
    c16, s16 = _rope_cs(pos, 16)
    one = lambda w: jnp.ones((n, w), F32)
    zero = lambda w: jnp.zeros((n, w), F32)
    cos_a = jnp.tile(c32, (1, 4))
    sin_a = jnp.tile(jnp.concatenate([-s32, s32], axis=1), (1, 2))
    cos_b = jnp.concatenate([one(64), c16, c16, one(32)], axis=1)
    sin_b = jnp.concatenate([zero(64), -s16, s16, zero(32)], axis=1)
    cos_c = jnp.concatenate([c16, c16, one(96)], axis=1)
    sin_c = jnp.concatenate([-s16, s16, zero(96)], axis=1)
    return jnp.concatenate([cos_a, sin_a, cos_b, sin_b, cos_c, sin_c], axis=1)


def _odd_table(pos):
    c64, s64 = _rope_cs(pos, 64)
    return jnp.concatenate([c64, c64, -s64, s64], axis=1)


_MOBA_HEAD_PERM = (0, 2, 1, 3, 4, 6, 5, 7)


def _even_weights(w_in, w_uq, w_uk, w_uv, w_out):
    hp = np.asarray(_MOBA_HEAD_PERM)
    qcols = (hp[:, None] * A_HEAD_DIM + np.arange(A_HEAD_DIM)[None, :]).reshape(-1)
    w_q = w_in[:, :512][:, qcols]
    w_in_p = jnp.concatenate(
        [w_q, w_in[:, 512:], jnp.zeros((D_MODEL, EVEN_W - w_in.shape[1]), F32)], axis=1).astype(BF16)
    uq = w_uq.reshape(MLA_Q_RANK, B_HEADS, MLA_D_NOPE + MLA_D_ROPE)
    w_uq_p = jnp.pad(uq, ((0, 0), (0, 0), (0, 32))).reshape(MLA_Q_RANK, B_HEADS * 128).astype(BF16)
    eye = jnp.eye(MLA_D_ROPE, dtype=F32)
    k_top = jnp.pad(w_uk, ((0, 0), (0, 0), (0, 64)))
    k_mid = jnp.broadcast_to(jnp.pad(eye, ((0, 0), (64, 32)))[:, None, :], (MLA_D_ROPE, B_HEADS, 128))
    k_all = jnp.concatenate([k_top, k_mid, jnp.zeros((96, B_HEADS, 128), F32)], axis=0).reshape(384, 1024)
    v_all = jnp.concatenate([w_uv.reshape(MLA_KV_RANK, 512), jnp.zeros((128, 512), F32)], axis=0)
    w_kv = jnp.concatenate([k_all, v_all], axis=1).astype(BF16)
    blocks = []
    for h in range(B_HEADS):
        blk = jnp.zeros((128, 384), F32)
        blk = blk.at[0:64, 0:256].set(w_uk[:, h, :].T)
        blk = blk.at[64:96, 256:288].set(eye)
        blocks.append(blk)
    w_abs = jax.scipy.linalg.block_diag(*blocks).astype(BF16)
    w_uv_bd = jax.scipy.linalg.block_diag(*[w_uv[:, h, :] for h in range(B_HEADS)]).astype(BF16)
    rows = (hp[:, None] * A_HEAD_DIM + np.arange(A_HEAD_DIM)[None, :]).reshape(-1)
    w_out_moba_p = w_out[:512][rows].astype(BF16)
    w_out_moba = w_out[:512].astype(BF16)
    w_out_mla = w_out[512:].astype(BF16)
    return w_in_p, w_uq_p, w_kv, w_abs, w_uv_bd, w_out_moba_p, w_out_moba, w_out_mla


def kernel(x_prompt, x_sample, mem_prompt, cache_moba_k, cache_moba_v, cache_mla, cache_swa0_kv, cache_swa1_kv, cache_swa2_kv, cache_mem_kv, page_table, ln_gain, ln_bias, w_in_even, w_out_even, mla_q_norm, mla_w_uq, mla_kv_norm, mla_w_uk, mla_w_uv, w_in_odd, w_out_odd, w_mem_q, w_mem_kv, w_mem_o, w_router, router_bias, w_exp_gate, w_exp_up, w_exp_down, w_sh_gate, w_sh_up, w_sh_down):
    bp, sp, _ = x_prompt.shape
    bs, ss, _ = x_sample.shape
    n_p = bp * sp
    n_s = bs * ss
    tm = 256
    n_pool = cache_moba_k.shape[1]
    swa_caches = (cache_swa0_kv, cache_swa1_kv, cache_swa2_kv)

    xp = x_prompt.reshape(n_p, D_MODEL)
    xs = x_sample.reshape(n_s, D_MODEL)
    pos_p = jnp.arange(sp)
    pos_s = jnp.tile(PAST_LEN + jnp.arange(ss), n_s // ss)
    tab_even_p, tab_even_s = _even_table(pos_p), _even_table(pos_s)
    tab_odd_p, tab_odd_s = _odd_table(pos_p), _odd_table(pos_s)
    mem2 = mem_prompt.reshape(bp * N_MEM, D_MODEL)

    ck_pages = cache_moba_k.reshape(-1, PAGE_SIZE, A_KV_HEADS * A_HEAD_DIM)
    cv_pages = cache_moba_v.reshape(-1, PAGE_SIZE, A_KV_HEADS * A_HEAD_DIM)
    cm_pages = cache_mla.reshape(-1, PAGE_SIZE, MLA_ROW)

    moba_k_p, moba_v_p, mla_p, mem_p = [], [], [], []
    moba_k_s, moba_v_s, mla_s = [], [], []
    swa_p = [[] for _ in range(N_SWA)]
    swa_s = [[] for _ in range(N_SWA)]

    for l in range(DEPTH):
        ln_g = [ln_gain[l, i].reshape(1, D_MODEL) for i in range(3)]
        ln_b = [ln_bias[l, i].reshape(1, D_MODEL) for i in range(3)]
        if l % 2 == 0:
            e = l // 2
            (w_in_p, w_uq_p, w_kv, w_abs, w_uv_bd, w_out_moba_p, w_out_moba, w_out_mla) = _even_weights(
                w_in_even[e], mla_w_uq[e], mla_w_uk[e], mla_w_uv[e], w_out_even[e])
            qn = mla_q_norm[e].reshape(1, MLA_Q_RANK)
            kvn = mla_kv_norm[e].reshape(1, MLA_KV_RANK)
            qa, ka, va, km, mla, qm, kx, vm = _even_proj(
                xp, tab_even_p, sp // tm, w_in_p, qn, w_uq_p, kvn, w_kv, tm)
            o_moba = _moba_prompt(qa, ka, va, km.reshape(n_p // MOBA_BLOCK, 256), bp, sp)
            o_mla = _mla_prompt(qm, kx, vm, bp, sp)
            xp = _mm_res_ln([o_moba, o_mla], [w_out_moba_p, w_out_mla], xp, ln_g[0], ln_b[0], tm)
            moba_k_p.append(ka.reshape(bp, sp, A_KV_HEADS, A_HEAD_DIM))
            moba_v_p.append(va.reshape(bp, sp, A_KV_HEADS, A_HEAD_DIM))
            mla_p.append(mla.reshape(bp, sp, MLA_ROW))
            qa, ka, va, km, mla, qm, kx, vm = _even_proj(
                xs, tab_even_s, n_s // tm, w_in_p, qn, w_uq_p, kvn, w_kv, tm)
            inv = np.argsort(np.asarray(_MOBA_HEAD_PERM))
            q5 = qa.reshape(bs, ss, A_HEADS, A_HEAD_DIM)[:, :, inv].reshape(bs, ss, A_KV_HEADS, 2, A_HEAD_DIM)
            eye_kv = jnp.eye(A_KV_HEADS, dtype=F32)[None, :, None, None, :, None]
            q_rows = (q5.transpose(0, 2, 1, 3, 4)[:, :, :, :, None, :] * eye_kv).reshape(
                bs, A_KV_HEADS * ss * 2, A_KV_HEADS * A_HEAD_DIM)
            o_ms = _moba_sample(page_table, q_rows, ka.reshape(bs, ss, 256), va.reshape(bs, ss, 256),
                                ck_pages, cv_pages, e * n_pool)
            o_ms = o_ms.reshape(bs, ss, 2, A_KV_HEADS, A_HEAD_DIM).transpose(0, 1, 3, 2, 4).reshape(n_s, 512)
            q_abs = _matmul(qm, w_abs, tm, BF16).reshape(bs, ss * B_HEADS, 384)
            o_lat = _mla_sample(page_table, q_abs, mla.reshape(bs, ss, MLA_ROW), cm_pages, e * n_pool)
            o_mls = _matmul(o_lat.reshape(n_s, B_HEADS * MLA_KV_RANK), w_uv_bd, tm)
            xs = _mm_res_ln([o_ms, o_mls], [w_out_moba, w_out_mla], xs, ln_g[0], ln_b[0], tm)
            moba_k_s.append(ka.reshape(bs, ss, A_KV_HEADS, A_HEAD_DIM))
            moba_v_s.append(va.reshape(bs, ss, A_KV_HEADS, A_HEAD_DIM))
            mla_s.append(mla.reshape(bs, ss, MLA_ROW))
        else:
            o = l // 2
            w_odd = w_in_odd[o].astype(BF16)
            w_oo = w_out_odd[o].astype(BF16)
            h_p = _odd_proj(xp, tab_odd_p, sp // tm, w_odd, tm)
            h3 = h_p.reshape(bp, sp, ODD_W)
            outs, lses = [], []
            for g, (win, dil) in enumerate(SWA_CONFIGS):
                og, lg = _dilated_prompt(h3, g, dil)
                outs.append(og)
                lses.append(lg)
                keep = min(win, sp)
                h6 = h3.reshape(bp, sp, N_SWA, 3, C_HEADS, C_HEAD_DIM)
                swa_p[g].append(h6[:, sp - keep:, g, 1:3])
            xp = _odd_out(outs, lses, w_oo, xp, ln_g[0], ln_b[0], tm)
            h_s = _odd_proj(xs, tab_odd_s, n_s // tm, w_odd, tm).reshape(bs, ss, ODD_W)
            views = [swa_caches[g].reshape(-1, SWA_KEYS, dil * 1024)
                     for g, (win, dil) in enumerate(SWA_CONFIGS)]
            outs_s, lses_s = _dilated_sample(h_s, *views, o * bs)
            xs = _odd_out([a.reshape(n_s, 512) for a in outs_s], [a.reshape(n_s, 128) for a in lses_s],
                          w_oo, xs, ln_g[0], ln_b[0], tm)
            h6 = h_s.reshape(bs, ss, N_SWA, 3, C_HEADS, C_HEAD_DIM)
            for g in range(N_SWA):
                swa_s[g].append(h6[:, :, g, 1:3])
        w_mq = w_mem_q[l].astype(BF16)
        w_mo = w_mem_o[l].astype(BF16)
        mkv = _matmul(mem2, w_mem_kv[l].astype(BF16), tm)
        mem_p.append(mkv.reshape(bp, N_MEM, 2, MEM_HEADS, MEM_HEAD_DIM))
        xp = _mem_prompt(xp, mkv.reshape(bp, N_MEM, 1024).astype(BF16), w_mq, w_mo,
                         ln_g[1], ln_b[1], sp // tm, tm)
        q_s = _matmul(xs, w_mq, tm).reshape(bs, ss, 512)
        att_s = _mem_sample_attn(q_s, cache_mem_kv.reshape(-1, N_MEM, 1024), l * bs)
        xs = _mm_res_ln([att_s.reshape(n_s, 512)], [w_mo], xs, ln_g[1], ln_b[1], tm)
        wr_t = w_router[l].T
        rb = router_bias[l].reshape(N_EXPERTS, 1)
        wg, wu, wd = w_exp_gate[l].astype(BF16), w_exp_up[l].astype(BF16), w_exp_down[l].astype(BF16)
        sg, su, sd = w_sh_gate[l].astype(BF16), w_sh_up[l].astype(BF16), w_sh_down[l].astype(BF16)
        gate_p = _router(xp, wr_t, rb, 512)
        xp = _moe(xp, gate_p, wg, wu, wd, sg, su, sd, ln_g[2], ln_b[2], 1024)
        gate_s = _router(xs, wr_t, rb, 512)
        xs = _moe(xs, gate_s, wg, wu, wd, sg, su, sd, ln_g[2], ln_b[2], 512)

    return (xp.reshape(bp, sp, D_MODEL), xs.reshape(bs, ss, D_MODEL),
            jnp.stack(moba_k_p), jnp.stack(moba_v_p), jnp.stack(mla_p),
            jnp.stack(swa_p[0]), jnp.stack(swa_p[1]), jnp.stack(swa_p[2]), jnp.stack(mem_p),
            jnp.stack(moba_k_s), jnp.stack(moba_v_s), jnp.stack(mla_s),
            jnp.stack(swa_s[0]), jnp.stack(swa_s[1]), jnp.stack(swa_s[2]))
```

```python
import functools

import numpy as np
import jax
import jax.numpy as jnp
from jax import lax
from jax.experimental import pallas as pl
from jax.experimental.pallas import tpu as pltpu

F32 = jnp.float32
BF16 = jnp.bfloat16
HIGHEST = lax.Precision.HIGHEST

D_MODEL = 1024
DEPTH = 4
PAST_LEN = 8192
PAGE_SIZE = 128
A_HEADS = 8
A_KV_HEADS = 4
A_HEAD_DIM = 64
MOBA_BLOCK = 256
MOBA_TOPK = 3
B_HEADS = 8
MLA_Q_RANK = 384
MLA_KV_RANK = 256
MLA_D_NOPE = 64
MLA_D_ROPE = 32
MLA_D_V = 64
MLA_ROW = MLA_KV_RANK + MLA_D_ROPE
SWA_CONFIGS = ((128, 1), (512, 4), (2048, 16))
N_SWA = 3
C_HEADS = 4
C_HEAD_DIM = 128
SWA_KEYS = 128
MLA_KV_TILE = 256
N_MEM = 256
MEM_HEADS = 4
MEM_HEAD_DIM = 128
N_EXPERTS = 64
N_EXPERT_GROUPS = 8
TOPK_GROUPS = 4
MOE_TOPK = 8
D_EXPERT = 256
ROUTED_SCALE = 2.5
ROPE_THETA = 10000.0
LN_EPS = 1e-5
RMS_EPS = 1e-6
NEG_INF = -1e30
ALPHA = (2 * DEPTH) ** 0.25
A_SCALE = A_HEAD_DIM ** -0.5
MLA_SCALE = (MLA_D_NOPE + MLA_D_ROPE) ** -0.5
C_SCALE = C_HEAD_DIM ** -0.5
MEM_SCALE = MEM_HEAD_DIM ** -0.5

LANES = 128
EVEN_W = 1792
ODD_W = N_SWA * 3 * C_HEADS * C_HEAD_DIM
VMEM_LIMIT = 56 * 1024 * 1024

_NT = (((1,), (1,)), ((), ()))


def _params(sem, vmem=VMEM_LIMIT):
    return pltpu.CompilerParams(dimension_semantics=sem, vmem_limit_bytes=vmem)


def _dot(a, b):
    return jnp.dot(a, b, preferred_element_type=F32)


def _dot_nt(a, b, precision=None):
    return lax.dot_general(a, b, _NT, precision=precision, preferred_element_type=F32)


def _layer_norm(z, g, b):
    mu = jnp.mean(z, axis=-1, keepdims=True)
    zc = z - mu
    var = jnp.mean(zc * zc, axis=-1, keepdims=True)
    return zc * lax.rsqrt(var + LN_EPS) * g + b


def _rms(x, g):
    return x * lax.rsqrt(jnp.mean(x * x, axis=-1, keepdims=True) + RMS_EPS) * g


def _sigmoid(x):
    return 1.0 / (1.0 + jnp.exp(-x))


def _top_k_mask(vals, k, axis):
    n = vals.shape[axis]
    idx = lax.broadcasted_iota(jnp.int32, vals.shape, axis).astype(F32)
    sel = jnp.zeros(vals.shape, F32)
    for _ in range(k):
        mx = jnp.max(vals, axis=axis, keepdims=True)
        first = jnp.min(jnp.where(vals == mx, idx, float(n)), axis=axis, keepdims=True)
        hit = idx == first
        sel = jnp.where(hit, 1.0, sel)
        vals = jnp.where(hit, -jnp.inf, vals)
    return sel


def _rope_block(yb, cos, sin, half, lane):
    fwd = pltpu.roll(yb, LANES - half, 1)
    bwd = pltpu.roll(yb, half, 1)
    first = (lane & (2 * half - 1)) < half
    return yb * cos + jnp.where(first, fwd, bwd) * sin


def _mm_kernel(x_ref, w_ref, o_ref):
    o_ref[...] = _dot(x_ref[...].astype(BF16), w_ref[...]).astype(o_ref.dtype)


def _matmul(x, w, tm, out_dtype=F32):
    m, k = x.shape
    n = w.shape[1]
    return pl.pallas_call(
        _mm_kernel,
        out_shape=jax.ShapeDtypeStruct((m, n), out_dtype),
        grid=(m // tm,),
        in_specs=[pl.BlockSpec((tm, k), lambda i: (i, 0)),
                  pl.BlockSpec((k, n), lambda i: (0, 0))],
        out_specs=pl.BlockSpec((tm, n), lambda i: (i, 0)),
        compiler_params=_params(("parallel",)),
        name="matmul",
    )(x, w)


def _mm_res_ln_kernel(n_in, *refs):
    a_refs = refs[:n_in]
    w_refs = refs[n_in:2 * n_in]
    r_ref, g_ref, b_ref, o_ref = refs[2 * n_in:]
    acc = None
    for a_ref, w_ref in zip(a_refs, w_refs):
        t = _dot(a_ref[...].astype(BF16), w_ref[...])
        acc = t if acc is None else acc + t
    o_ref[...] = _layer_norm(ALPHA * r_ref[...] + acc, g_ref[...], b_ref[...])


def _mm_res_ln(a_list, w_list, resid, g, b, tm):
    m = resid.shape[0]
    n_in = len(a_list)
    in_specs = [pl.BlockSpec((tm, a.shape[1]), lambda i: (i, 0)) for a in a_list]
    in_specs += [pl.BlockSpec(w.shape, lambda i: (0, 0)) for w in w_list]
    in_specs += [pl.BlockSpec((tm, D_MODEL), lambda i: (i, 0)),
                 pl.BlockSpec((1, D_MODEL), lambda i: (0, 0)),
                 pl.BlockSpec((1, D_MODEL), lambda i: (0, 0))]
    return pl.pallas_call(
        functools.partial(_mm_res_ln_kernel, n_in),
        out_shape=jax.ShapeDtypeStruct((m, D_MODEL), F32),
        grid=(m // tm,),
        in_specs=in_specs,
        out_specs=pl.BlockSpec((tm, D_MODEL), lambda i: (i, 0)),
        compiler_params=_params(("parallel",)),
        name="mm_res_ln",
    )(*a_list, *w_list, resid, g, b)


def _even_proj_kernel(x_ref, tab_ref, win_ref, qn_ref, wuq_ref, kvn_ref, wkv_ref,
                      qa_ref, ka_ref, va_ref, km_ref, mla_ref, qm_ref, kx_ref, vm_ref):
    tm = x_ref.shape[0]
    lane = lax.broadcasted_iota(jnp.int32, (tm, LANES), 1)
    y = _dot(x_ref[...].astype(BF16), win_ref[...])
    cos_a, sin_a = tab_ref[:, 0:128], tab_ref[:, 128:256]
    cos_b, sin_b = tab_ref[:, 256:384], tab_ref[:, 384:512]
    cos_c, sin_c = tab_ref[:, 512:640], tab_ref[:, 640:768]
    for j in range(4):
        qa_ref[:, j * 128:(j + 1) * 128] = _rope_block(
            y[:, j * 128:(j + 1) * 128], cos_a, sin_a, 32, lane)
    for j in range(2):
        kb = _rope_block(y[:, 512 + j * 128:640 + j * 128], cos_a, sin_a, 32, lane)
        ka_ref[:, j * 128:(j + 1) * 128] = kb
        km_ref[0, :, j * 128:(j + 1) * 128] = jnp.mean(kb, axis=0, keepdims=True)
    va_ref[...] = y[:, 768:1024]
    cq = _rms(y[:, 1024:1408], qn_ref[...])
    qm = _dot(cq.astype(BF16), wuq_ref[...])
    for j in range(B_HEADS):
        qj = _rope_block(qm[:, j * 128:(j + 1) * 128], cos_b, sin_b, 16, lane)
        qm_ref[:, j * 128:(j + 1) * 128] = (qj * MLA_SCALE).astype(BF16)
    ckv = _rms(y[:, 1408:1664], kvn_ref[...])
    kpe = _rope_block(y[:, 1664:1792], cos_c, sin_c, 16, lane)
    mla_ref[:, 0:MLA_KV_RANK] = ckv
    mla_ref[:, MLA_KV_RANK:MLA_ROW] = kpe[:, 0:MLA_D_ROPE]
    ck = jnp.concatenate([ckv, kpe], axis=1).astype(BF16)
    kv = _dot(ck, wkv_ref[...])
    kx_ref[...] = kv[:, 0:1024].astype(BF16)
    vm_ref[...] = kv[:, 1024:1536].astype(BF16)


def _even_proj(x, tab, tab_blocks, w_in, q_norm, w_uq, kv_norm, w_kv, tm):
    m = x.shape[0]
    nt = m // tm
    row = lambda i: (i, 0)
    const = lambda i: (0, 0)
    outs = (
        jax.ShapeDtypeStruct((m, 512), F32),
        jax.ShapeDtypeStruct((m, 256), F32),
        jax.ShapeDtypeStruct((m, 256), F32),
        jax.ShapeDtypeStruct((nt, 1, 256), F32),
        jax.ShapeDtypeStruct((m, MLA_ROW), F32),
        jax.ShapeDtypeStruct((m, 1024), BF16),
        jax.ShapeDtypeStruct((m, 1024), BF16),
        jax.ShapeDtypeStruct((m, 512), BF16),
    )
    return pl.pallas_call(
        _even_proj_kernel,
        out_shape=outs,
        grid=(nt,),
        in_specs=[pl.BlockSpec((tm, D_MODEL), row),
                  pl.BlockSpec((tm, 768), lambda i: (i % tab_blocks, 0)),
                  pl.BlockSpec(w_in.shape, const),
                  pl.BlockSpec(q_norm.shape, const),
                  pl.BlockSpec(w_uq.shape, const),
                  pl.BlockSpec(kv_norm.shape, const),
                  pl.BlockSpec(w_kv.shape, const)],
        out_specs=(pl.BlockSpec((tm, 512), row),
                   pl.BlockSpec((tm, 256), row),
                   pl.BlockSpec((tm, 256), row),
                   pl.BlockSpec((1, 1, 256), lambda i: (i, 0, 0)),
                   pl.BlockSpec((tm, MLA_ROW), row),
                   pl.BlockSpec((tm, 1024), row),
                   pl.BlockSpec((tm, 1024), row),
                   pl.BlockSpec((tm, 512), row)),
        compiler_params=_params(("parallel",)),
        name="even_proj",
    )(x, tab, w_in, q_norm, w_uq, kv_norm, w_kv)


_TN = (((0,), (0,)), ((), ()))


def _dot_tn(a, b):
    return lax.dot_general(a, b, _TN, preferred_element_type=F32)


def _moba_p_kernel(q_ref, k_ref, v_ref, km_ref, o_ref, sel_s, m_s, l_s, acc_s):
    tq = MOBA_BLOCK
    c = pl.program_id(2)
    n_blk = km_ref.shape[0]
    lane = lax.broadcasted_iota(jnp.int32, (tq, LANES), 1)
    lo = lane < A_HEAD_DIM
    q_a = q_ref[:, 0:128]
    q_b = q_ref[:, 128:256]
    q4 = jnp.concatenate([jnp.where(lo, q_a, 0.0), jnp.where(lo, 0.0, q_a),
                          jnp.where(lo, q_b, 0.0), jnp.where(lo, 0.0, q_b)], axis=0)
    gate = _dot_nt(km_ref[...], q4, precision=HIGHEST)
    past = lax.broadcasted_iota(jnp.int32, gate.shape, 0) < c
    sel = _top_k_mask(jnp.where(past, gate, NEG_INF), min(MOBA_TOPK, n_blk), 0)
    sel_s[...] = jnp.where(past, sel, 0.0)

    qb = (q4 * A_SCALE).astype(BF16)

    def kv_tile(j):
        start = pl.multiple_of(j * tq, tq)
        return (k_ref[pl.ds(start, tq), :].astype(BF16), v_ref[pl.ds(start, tq), :].astype(BF16))

    k_own, v_own = kv_tile(c)
    s = _dot_nt(k_own, qb)
    key = lax.broadcasted_iota(jnp.int32, s.shape, 0)
    qpos = lax.broadcasted_iota(jnp.int32, s.shape, 1) & (tq - 1)
    s = jnp.where(key <= qpos, s, NEG_INF)
    m0 = jnp.max(s, axis=0, keepdims=True)
    p = jnp.exp(s - m0)
    m_s[...] = m0
    l_s[...] = jnp.sum(p, axis=0, keepdims=True)
    acc_s[...] = _dot_tn(v_own, p.astype(BF16))

    def body(j, carry):
        kj, vj = kv_tile(j)
        chosen = sel_s[pl.ds(j, 1), :] > 0.5
        sj = jnp.where(chosen, _dot_nt(kj, qb), NEG_INF)
        m_old = m_s[...]
        m_new = jnp.maximum(m_old, jnp.max(sj, axis=0, keepdims=True))
        a = jnp.exp(m_old - m_new)
        pj = jnp.exp(sj - m_new)
        m_s[...] = m_new
        l_s[...] = a * l_s[...] + jnp.sum(pj, axis=0, keepdims=True)
        acc_s[...] = a * acc_s[...] + _dot_tn(vj, pj.astype(BF16))
        return carry

    lax.fori_loop(0, c, body, 0)
    o = acc_s[...] / l_s[...]
    top = lax.broadcasted_iota(jnp.int32, (LANES, tq), 0) < A_HEAD_DIM
    o_ref[:, 0:128] = jnp.where(top, o[:, 0:tq], o[:, tq:2 * tq]).T
    o_ref[:, 128:256] = jnp.where(top, o[:, 2 * tq:3 * tq], o[:, 3 * tq:4 * tq]).T


def _moba_prompt(qa, ka, va, kmean, bsz, seq):
    tq = MOBA_BLOCK
    nc = seq // tq
    return pl.pallas_call(
        _moba_p_kernel,
        out_shape=jax.ShapeDtypeStruct((bsz * seq, 512), F32),
        grid=(bsz, 2, nc),
        in_specs=[pl.BlockSpec((tq, 256), lambda b, p, c: (b * nc + c, p)),
                  pl.BlockSpec((seq, 128), lambda b, p, c: (b, p)),
                  pl.BlockSpec((seq, 128), lambda b, p, c: (b, p)),
                  pl.BlockSpec((nc, 128), lambda b, p, c: (b, p))],
        out_specs=pl.BlockSpec((tq, 256), lambda b, p, c: (b * nc + c, p)),
        scratch_shapes=[pltpu.VMEM((nc, 4 * tq), F32), pltpu.VMEM((1, 4 * tq), F32),
                        pltpu.VMEM((1, 4 * tq), F32), pltpu.VMEM((128, 4 * tq), F32)],
        compiler_params=_params(("parallel", "parallel", "arbitrary")),
        name="moba_prompt",
    )(qa, ka, va, kmean)


def _mla_p_kernel(q_ref, k_ref, v_ref, o_ref):
    tq = q_ref.shape[0]
    tk = MLA_KV_TILE
    per_q = tq // tk
    c = pl.program_id(2)
    q0 = q_ref[:, 0:128]
    q1 = q_ref[:, 128:256]
    top = lax.broadcasted_iota(jnp.int32, (LANES, tq), 0) < MLA_D_V

    def tile(j, diag_offset):
        start = pl.multiple_of(j * tk, tk)
        k = k_ref[pl.ds(start, tk), :]
        v = v_ref[pl.ds(start, tk), :]
        s0 = _dot_nt(k[:, 0:128], q0)
        s1 = _dot_nt(k[:, 128:256], q1)
        if diag_offset is not None:
            key = lax.broadcasted_iota(jnp.int32, s0.shape, 0) + diag_offset
            qpos = lax.broadcasted_iota(jnp.int32, s0.shape, 1)
            s0 = jnp.where(key <= qpos, s0, NEG_INF)
            s1 = jnp.where(key <= qpos, s1, NEG_INF)
        return s0, s1, v

    def update(j, diag_offset, state):
        m0, m1, l0, l1, acc = state
        s0, s1, v = tile(j, diag_offset)
        m0n = jnp.maximum(m0, jnp.max(s0, axis=0, keepdims=True))
        m1n = jnp.maximum(m1, jnp.max(s1, axis=0, keepdims=True))
        a0 = jnp.exp(m0 - m0n)
        a1 = jnp.exp(m1 - m1n)
        p0 = jnp.exp(s0 - m0n)
        p1 = jnp.exp(s1 - m1n)
        l0 = a0 * l0 + jnp.sum(p0, axis=0, keepdims=True)
        l1 = a1 * l1 + jnp.sum(p1, axis=0, keepdims=True)
        pv = jnp.where(top, _dot_tn(v, p0.astype(BF16)), _dot_tn(v, p1.astype(BF16)))
        acc = jnp.where(top, a0, a1) * acc + pv
        return m0n, m1n, l0, l1, acc

    neg = jnp.full((1, tq), NEG_INF, F32)
    zero = jnp.zeros((1, tq), F32)
    state = (neg, neg, zero, zero, jnp.zeros((LANES, tq), F32))
    for d in range(per_q):
        state = update(c * per_q + d, d * tk, state)
    m0, m1, l0, l1, acc = lax.fori_loop(0, c * per_q, lambda j, st: update(j, None, st), state)
    o_ref[...] = (acc / jnp.where(top, l0, l1)).T


def _mla_prompt(qm, kx, vm, bsz, seq, tq=512):
    nc = seq // tq
    return pl.pallas_call(
        _mla_p_kernel,
        out_shape=jax.ShapeDtypeStruct((bsz * seq, 512), F32),
        grid=(bsz, B_HEADS // 2, nc),
        in_specs=[pl.BlockSpec((tq, 256), lambda b, h, c: (b * nc + c, h)),
                  pl.BlockSpec((seq, 256), lambda b, h, c: (b, h)),
                  pl.BlockSpec((seq, 128), lambda b, h, c: (b, h))],
        out_specs=pl.BlockSpec((tq, 128), lambda b, h, c: (b * nc + c, h)),
        compiler_params=_params(("parallel", "parallel", "arbitrary")),
        name="mla_prompt",
    )(qm, kx, vm)


def _page_copies(pt_ref, cache_hbm, buf, sem, bb, slot, n_pages, page_base):
    copies = []
    for j in range(n_pages):
        pg = pt_ref[bb, j] + page_base
        copies.append(pltpu.make_async_copy(
            cache_hbm.at[pg], buf.at[slot, :, pl.ds(j * PAGE_SIZE, PAGE_SIZE)], sem.at[slot]))
    return copies


def _prefetch_pages(pt_ref, caches, bufs, sems, n_pages, page_base):
    b = pl.program_id(0)
    nb = pl.num_programs(0)
    slot = b % 2

    def start(bb, sl):
        for cache, buf, sem in zip(caches, bufs, sems):
            for cp in _page_copies(pt_ref, cache, buf, sem, bb, sl, n_pages, page_base):
                cp.start()

    @pl.when(b == 0)
    def _():
        start(0, 0)

    @pl.when(b + 1 < nb)
    def _():
        start(b + 1, 1 - slot)

    for cache, buf, sem in zip(caches, bufs, sems):
        for cp in _page_copies(pt_ref, cache, buf, sem, b, slot, n_pages, page_base):
            cp.wait()
    return slot


def _moba_s_kernel(page_base, pt_ref, q_ref, kn_ref, vn_ref, kc_hbm, vc_hbm, o_ref,
                   kbuf, vbuf, s_s, ksem, vsem):
    n_keys = kbuf.shape[2]
    n_pages = n_keys // PAGE_SIZE
    n_blk = n_keys // MOBA_BLOCK
    chunk = 1024
    n_chunk = n_keys // chunk
    slot = _prefetch_pages(pt_ref, (kc_hbm, vc_hbm), (kbuf, vbuf), (ksem, vsem), n_pages, page_base)

    q = q_ref[0]
    qb = (q * A_SCALE).astype(BF16)
    feat = kbuf.shape[1]
    blk_lane = lax.broadcasted_iota(jnp.int32, (feat, LANES), 1)
    km_t = jnp.zeros((feat, LANES), F32)
    for n in range(n_blk):
        mean_n = jnp.mean(kbuf[slot, :, n * MOBA_BLOCK:(n + 1) * MOBA_BLOCK], axis=1, keepdims=True)
        km_t = jnp.where(blk_lane == n, mean_n, km_t)
    for ci in range(n_chunk):
        kc = kbuf[slot, :, ci * chunk:(ci + 1) * chunk]
        s_s[:, ci * chunk:(ci + 1) * chunk] = _dot(qb, kc.astype(BF16))
    gate = jnp.dot(q, km_t, precision=HIGHEST, preferred_element_type=F32)[:, 0:n_blk]
    sel = _top_k_mask(gate, min(MOBA_TOPK, n_blk), 1)
    key_blk = lax.broadcasted_iota(jnp.int32, (n_blk, n_keys), 1) >> 8
    expand = jnp.where(key_blk == lax.broadcasted_iota(jnp.int32, (n_blk, n_keys), 0), 1.0, 0.0)
    chosen = _dot(sel.astype(BF16), expand.astype(BF16)) > 0.5
    s = jnp.where(chosen, s_s[...], NEG_INF)

    s_new = _dot_nt(qb, kn_ref[0].astype(BF16))
    tok = (lax.broadcasted_iota(jnp.int32, s_new.shape, 0) & 7) >> 1
    s_new = jnp.where(lax.broadcasted_iota(jnp.int32, s_new.shape, 1) <= tok, s_new, NEG_INF)
    m = jnp.maximum(jnp.max(s, axis=1, keepdims=True), jnp.max(s_new, axis=1, keepdims=True))
    p = jnp.exp(s - m)
    p_new = jnp.exp(s_new - m)
    l = jnp.sum(p, axis=1, keepdims=True) + jnp.sum(p_new, axis=1, keepdims=True)
    acc = _dot(p_new.astype(BF16), vn_ref[0].astype(BF16))
    pb = p.astype(BF16)
    for ci in range(n_chunk):
        vc = vbuf[slot, :, ci * chunk:(ci + 1) * chunk].astype(BF16)
        acc = acc + _dot_nt(pb[:, ci * chunk:(ci + 1) * chunk], vc)
    o = acc / l
    lane_head = lax.broadcasted_iota(jnp.int32, (8, o.shape[1]), 1) >> 6
    out = jnp.zeros((8, o.shape[1]), F32)
    for k in range(A_KV_HEADS):
        out = out + jnp.where(lane_head == k, o[8 * k:8 * k + 8], 0.0)
    o_ref[0] = out


def _moba_sample(page_table, q_rows, k_new, v_new, cache_k, cache_v, page_base):
    dbs, n_pages = page_table.shape
    n_keys = n_pages * PAGE_SIZE
    grid_spec = pltpu.PrefetchScalarGridSpec(
        num_scalar_prefetch=1,
        grid=(dbs,),
        in_specs=[pl.BlockSpec((1,) + q_rows.shape[1:], lambda b, pt: (b, 0, 0)),
                  pl.BlockSpec((1,) + k_new.shape[1:], lambda b, pt: (b, 0, 0)),
                  pl.BlockSpec((1,) + v_new.shape[1:], lambda b, pt: (b, 0, 0)),
                  pl.BlockSpec(memory_space=pl.ANY),
                  pl.BlockSpec(memory_space=pl.ANY)],
        out_specs=pl.BlockSpec((1, 8, 256), lambda b, pt: (b, 0, 0)),
        scratch_shapes=[pltpu.VMEM((2, 256, n_keys), F32), pltpu.VMEM((2, 256, n_keys), F32),
                        pltpu.VMEM((q_rows.shape[1], n_keys), F32),
                        pltpu.SemaphoreType.DMA((2,)), pltpu.SemaphoreType.DMA((2,))],
    )
    return pl.pallas_call(
        functools.partial(_moba_s_kernel, page_base),
        out_shape=jax.ShapeDtypeStruct((dbs, 8, 256), F32),
        grid_spec=grid_spec,
        compiler_params=_params(("arbitrary",), 60 * 1024 * 1024),
        name="moba_sample",
    )(page_table, q_rows, k_new, v_new, cache_k, cache_v)


def _mla_s_kernel(page_base, pt_ref, q_ref, new_ref, c_hbm, o_ref, cbuf, s_s, csem):
    n_keys = cbuf.shape[2]
    n_pages = n_keys // PAGE_SIZE
    chunk = 1024
    n_chunk = n_keys // chunk
    slot = _prefetch_pages(pt_ref, (c_hbm,), (cbuf,), (csem,), n_pages, page_base)

    q = q_ref[0][:, 0:MLA_ROW]
    for ci in range(n_chunk):
        pc = cbuf[slot, :, ci * chunk:(ci + 1) * chunk].astype(BF16)
        s_s[:, ci * chunk:(ci + 1) * chunk] = _dot(q, pc)
    s = s_s[...]
    new = new_ref[0]
    s_new = _dot_nt(q, new.astype(BF16))
    tok = lax.broadcasted_iota(jnp.int32, s_new.shape, 0) >> 3
    s_new = jnp.where(lax.broadcasted_iota(jnp.int32, s_new.shape, 1) <= tok, s_new, NEG_INF)
    m = jnp.maximum(jnp.max(s, axis=1, keepdims=True), jnp.max(s_new, axis=1, keepdims=True))
    p = jnp.exp(s - m)
    p_new = jnp.exp(s_new - m)
    l = jnp.sum(p, axis=1, keepdims=True) + jnp.sum(p_new, axis=1, keepdims=True)
    acc = _dot(p_new.astype(BF16), new[:, 0:MLA_KV_RANK].astype(BF16))
    pb = p.astype(BF16)
    for ci in range(n_chunk):
        cc = cbuf[slot, 0:MLA_KV_RANK, ci * chunk:(ci + 1) * chunk].astype(BF16)
        acc = acc + _dot_nt(pb[:, ci * chunk:(ci + 1) * chunk], cc)
    o_ref[0] = acc / l


def _mla_sample(page_table, q_rows, new_rows, cache, page_base):
    dbs, n_pages = page_table.shape
    n_keys = n_pages * PAGE_SIZE
    grid_spec = pltpu.PrefetchScalarGridSpec(
        num_scalar_prefetch=1,
        grid=(dbs,),
        in_specs=[pl.BlockSpec((1,) + q_rows.shape[1:], lambda b, pt: (b, 0, 0)),
                  pl.BlockSpec((1,) + new_rows.shape[1:], lambda b, pt: (b, 0, 0)),
                  pl.BlockSpec(memory_space=pl.ANY)],
        out_specs=pl.BlockSpec((1, q_rows.shape[1], MLA_KV_RANK), lambda b, pt: (b, 0, 0)),
        scratch_shapes=[pltpu.VMEM((2, MLA_ROW, n_keys), F32),
                        pltpu.VMEM((q_rows.shape[1], n_keys), F32),
                        pltpu.SemaphoreType.DMA((2,))],
    )
    return pl.pallas_call(
        functools.partial(_mla_s_kernel, page_base),
        out_shape=jax.ShapeDtypeStruct((dbs, q_rows.shape[1], MLA_KV_RANK), F32),
        grid_spec=grid_spec,
        compiler_params=_params(("arbitrary",), 60 * 1024 * 1024),
        name="mla_sample",
    )(page_table, q_rows, new_rows, cache)


def _odd_proj_kernel(x_ref, tab_ref, w_ref, o_ref):
    tm = x_ref.shape[0]
    lane = lax.broadcasted_iota(jnp.int32, (tm, LANES), 1)
    cos, sin = tab_ref[:, 0:128], tab_ref[:, 128:256]
    xb = x_ref[...].astype(BF16)
    for g in range(N_SWA):
        base = g * 1536
        y = _dot(xb, w_ref[:, base:base + 1536])
        for j in range(8):
            o_ref[:, base + j * 128:base + (j + 1) * 128] = _rope_block(
                y[:, j * 128:(j + 1) * 128], cos, sin, 64, lane)
        o_ref[:, base + 1024:base + 1536] = y[:, 1024:1536]


def _odd_proj(x, tab, tab_blocks, w, tm):
    m = x.shape[0]
    return pl.pallas_call(
        _odd_proj_kernel,
        out_shape=jax.ShapeDtypeStruct((m, ODD_W), F32),
        grid=(m // tm,),
        in_specs=[pl.BlockSpec((tm, D_MODEL), lambda i: (i, 0)),
                  pl.BlockSpec((tm, 256), lambda i: (i % tab_blocks, 0)),
                  pl.BlockSpec(w.shape, lambda i: (0, 0))],
        out_specs=pl.BlockSpec((tm, ODD_W), lambda i: (i, 0)),
        compiler_params=_params(("parallel",)),
        name="odd_proj",
    )(x, tab, w)


def _dil_p_kernel(q_ref, kp_ref, kc_ref, vp_ref, vc_ref, o_ref, lse_ref):
    tq = SWA_KEYS
    mt = pl.program_id(2)
    i = lax.broadcasted_iota(jnp.int32, (tq, 2 * tq), 0)
    kk = lax.broadcasted_iota(jnp.int32, (tq, 2 * tq), 1)
    first_ok = jnp.where(mt > 0, 0, tq)
    band = (kk >= i) & (kk <= i + tq) & (kk >= first_ok)
    lane = lax.broadcasted_iota(jnp.int32, (tq, LANES), 1)
    stats = jnp.zeros((tq, LANES), F32)
    for h in range(C_HEADS):
        hs = slice(h * 128, (h + 1) * 128)
        q = (q_ref[0, :, hs] * C_SCALE).astype(BF16)
        k = jnp.concatenate([kp_ref[0, :, hs], kc_ref[0, :, hs]], axis=0).astype(BF16)
        v = jnp.concatenate([vp_ref[0, :, hs], vc_ref[0, :, hs]], axis=0).astype(BF16)
        s = jnp.where(band, _dot_nt(q, k), NEG_INF)
        m = jnp.max(s, axis=1, keepdims=True)
        p = jnp.exp(s - m)
        l = jnp.sum(p, axis=1, keepdims=True)
        o_ref[0, :, hs] = _dot(p.astype(BF16), v) / l
        stats = jnp.where(lane == h, m + jnp.log(l), stats)
    lse_ref[0] = stats


def _dilated_prompt(h3, g, dil):
    bsz, seq, _ = h3.shape
    sd = seq // dil
    tq = SWA_KEYS
    nt = sd // tq
    hv = h3.reshape(bsz, sd, dil * ODD_W)
    blocks_per_pos = ODD_W // 512
    qi, ki, vi = 3 * g, 3 * g + 1, 3 * g + 2
    prev = lambda mt: jnp.maximum(mt - 1, 0)
    o, lse = pl.pallas_call(
        _dil_p_kernel,
        out_shape=(jax.ShapeDtypeStruct((bsz, sd, dil * 512), F32),
                   jax.ShapeDtypeStruct((bsz, sd, dil * 128), F32)),
        grid=(bsz, dil, nt),
        in_specs=[pl.BlockSpec((1, tq, 512), lambda b, r, mt: (b, mt, r * blocks_per_pos + qi)),
                  pl.BlockSpec((1, tq, 512), lambda b, r, mt: (b, prev(mt), r * blocks_per_pos + ki)),
                  pl.BlockSpec((1, tq, 512), lambda b, r, mt: (b, mt, r * blocks_per_pos + ki)),
                  pl.BlockSpec((1, tq, 512), lambda b, r, mt: (b, prev(mt), r * blocks_per_pos + vi)),
                  pl.BlockSpec((1, tq, 512), lambda b, r, mt: (b, mt, r * blocks_per_pos + vi))],
        out_specs=(pl.BlockSpec((1, tq, 512), lambda b, r, mt: (b, mt, r)),
                   pl.BlockSpec((1, tq, 128), lambda b, r, mt: (b, mt, r))),
        compiler_params=_params(("parallel", "parallel", "arbitrary")),
        name="dilated_prompt",
    )(hv, hv, hv, hv, hv)
    return o.reshape(bsz * seq, 512), lse.reshape(bsz * seq, 128)


def _dil_s_kernel(q0_ref, q1_ref, q2_ref, n0_ref, n1_ref, n2_ref, c0_ref, c1_ref, c2_ref,
                  o0_ref, o1_ref, o2_ref, l0_ref, l1_ref, l2_ref):
    q_refs = (q0_ref, q1_ref, q2_ref)
    n_refs = (n0_ref, n1_ref, n2_ref)
    c_refs = (c0_ref, c1_ref, c2_ref)
    o_refs = (o0_ref, o1_ref, o2_ref)
    l_refs = (l0_ref, l1_ref, l2_ref)
    for g in range(N_SWA):
        q = (q_refs[g][0] * C_SCALE).astype(BF16)
        cache = c_refs[g][0]
        n_old = cache.shape[0] * cache.shape[1]
        rows = jnp.concatenate([cache.reshape(n_old, C_HEAD_DIM), n_refs[g][0]], axis=0).astype(BF16)
        s = _dot_nt(q, rows)
        col = lax.broadcasted_iota(jnp.int32, s.shape, 1)
        row = lax.broadcasted_iota(jnp.int32, s.shape, 0)
        tok, head = row >> 2, row & 3
        new_col = col - n_old
        is_old = col < n_old
        if g == 0:
            ok_old = ((col & 7) == head) & ((col >> 3) >= tok)
            ok_new = ((new_col & 7) == head) & ((new_col >> 3) <= tok)
        else:
            ok_old = (col & 31) == tok * 8 + head
            ok_new = new_col == tok * 8 + head
        s = jnp.where((is_old & ok_old) | ((col >= n_old) & ok_new), s, NEG_INF)
        m = jnp.max(s, axis=1, keepdims=True)
        p = jnp.exp(s - m)
        l = jnp.sum(p, axis=1, keepdims=True)
        pv = pltpu.roll(p, 4, 1).astype(BF16)
        o_refs[g][0] = _dot(pv, rows) / l
        l_refs[g][0] = jnp.broadcast_to(m + jnp.log(l), (q.shape[0], LANES))


def _dilated_sample(q_list, new_list, caches, row_base):
    dbs = q_list[0].shape[0]
    sds = jax.ShapeDtypeStruct((dbs, 16, LANES), F32)
    idx = lambda b: (b, 0, 0)
    cidx = lambda b: (row_base + b, 0, 0, 0)
    cache_rows = (8, 32, 32)
    outs = pl.pallas_call(
        _dil_s_kernel,
        out_shape=(sds,) * 6,
        grid=(dbs,),
        in_specs=[pl.BlockSpec((1, 16, LANES), idx)] * 3 + [pl.BlockSpec((1, LANES, LANES), idx)] * 3
        + [pl.BlockSpec((1, SWA_KEYS, r, C_HEAD_DIM), cidx) for r in cache_rows],
        out_specs=(pl.BlockSpec((1, 16, LANES), idx),) * 6,
        compiler_params=_params(("parallel",)),
        name="dilated_sample",
    )(*q_list, *new_list, *caches)
    return outs[:3], outs[3:]


def _odd_out_kernel(o0_ref, o1_ref, o2_ref, l0_ref, l1_ref, l2_ref, w_ref, r_ref, g_ref, b_ref, out_ref):
    parts = []
    for h in range(C_HEADS):
        hs = slice(h * 128, (h + 1) * 128)
        a0, a1, a2 = l0_ref[:, h:h + 1], l1_ref[:, h:h + 1], l2_ref[:, h:h + 1]
        mx = jnp.maximum(jnp.maximum(a0, a1), a2)
        e0, e1, e2 = jnp.exp(a0 - mx), jnp.exp(a1 - mx), jnp.exp(a2 - mx)
        den = e0 + e1 + e2
        parts.append((e0 / den) * o0_ref[:, hs] + (e1 / den) * o1_ref[:, hs] + (e2 / den) * o2_ref[:, hs])
    comb = jnp.concatenate(parts, axis=1).astype(BF16)
    out_ref[...] = _layer_norm(ALPHA * r_ref[...] + _dot(comb, w_ref[...]), g_ref[...], b_ref[...])


def _odd_out(outs, lses, w, resid, g, b, tm):
    m = resid.shape[0]
    row = lambda i: (i, 0)
    const = lambda i: (0, 0)
    return pl.pallas_call(
        _odd_out_kernel,
        out_shape=jax.ShapeDtypeStruct((m, D_MODEL), F32),
        grid=(m // tm,),
        in_specs=[pl.BlockSpec((tm, 512), row)] * 3 + [pl.BlockSpec((tm, 128), row)] * 3
        + [pl.BlockSpec(w.shape, const), pl.BlockSpec((tm, D_MODEL), row),
           pl.BlockSpec((1, D_MODEL), const), pl.BlockSpec((1, D_MODEL), const)],
        out_specs=pl.BlockSpec((tm, D_MODEL), row),
        compiler_params=_params(("parallel",)),
        name="odd_out",
    )(*outs, *lses, w, resid, g, b)


def _mem_heads(q, kv_ref_val):
    outs = []
    for h in range(MEM_HEADS):
        k = kv_ref_val[:, h * 128:(h + 1) * 128]
        v = kv_ref_val[:, 512 + h * 128:512 + (h + 1) * 128]
        s = _dot_nt(q[:, h * 128:(h + 1) * 128].astype(BF16), k)
        m = jnp.max(s, axis=1, keepdims=True)
        p = jnp.exp(s - m)
        l = jnp.sum(p, axis=1, keepdims=True)
        outs.append(_dot(p.astype(BF16), v) / l)
    return jnp.concatenate(outs, axis=1)


def _mem_p_kernel(x_ref, kv_ref, wq_ref, wo_ref, g_ref, b_ref, o_ref):
    x = x_ref[...]
    q = _dot(x.astype(BF16), wq_ref[...]) * MEM_SCALE
    att = _mem_heads(q, kv_ref[0])
    o_ref[...] = _layer_norm(ALPHA * x + _dot(att.astype(BF16), wo_ref[...]), g_ref[...], b_ref[...])


def _mem_prompt(x, kv, wq, wo, g, b, tiles_per_batch, tm):
    m = x.shape[0]
    row = lambda i: (i, 0)
    const = lambda i: (0, 0)
    return pl.pallas_call(
        _mem_p_kernel,
        out_shape=jax.ShapeDtypeStruct((m, D_MODEL), F32),
        grid=(m // tm,),
        in_specs=[pl.BlockSpec((tm, D_MODEL), row),
                  pl.BlockSpec((1, N_MEM, 1024), lambda i: (i // tiles_per_batch, 0, 0)),
                  pl.BlockSpec(wq.shape, const), pl.BlockSpec(wo.shape, const),
                  pl.BlockSpec((1, D_MODEL), const), pl.BlockSpec((1, D_MODEL), const)],
        out_specs=pl.BlockSpec((tm, D_MODEL), row),
        compiler_params=_params(("parallel",)),
        name="mem_prompt",
    )(x, kv, wq, wo, g, b)


def _mem_s_kernel(q_ref, kv_ref, o_ref):
    rows = kv_ref[0].astype(BF16)
    q = (q_ref[0] * MEM_SCALE).astype(BF16)
    s = _dot_nt(q, rows)
    col = lax.broadcasted_iota(jnp.int32, s.shape, 1)
    head = lax.broadcasted_iota(jnp.int32, s.shape, 0) & 3
    s = jnp.where((col & 7) == head, s, NEG_INF)
    m = jnp.max(s, axis=1, keepdims=True)
    p = jnp.exp(s - m)
    l = jnp.sum(p, axis=1, keepdims=True)
    pv = pltpu.roll(p, 4, 1).astype(BF16)
    o_ref[0] = _dot(pv, rows) / l


def _mem_sample_attn(q, kv, row_base):
    dbs = q.shape[0]
    idx = lambda b: (b, 0, 0)
    return pl.pallas_call(
        _mem_s_kernel,
        out_shape=jax.ShapeDtypeStruct(q.shape, F32),
        grid=(dbs,),
        in_specs=[pl.BlockSpec((1,) + q.shape[1:], idx),
                  pl.BlockSpec((1,) + kv.shape[1:], lambda b: (row_base + b, 0, 0))],
        out_specs=pl.BlockSpec((1,) + q.shape[1:], idx),
        compiler_params=_params(("parallel",)),
        name="mem_sample",
    )(q, kv)


def _router_kernel(x_ref, wr_ref, rb_ref, gate_ref):
    tm = x_ref.shape[0]
    per_grp = N_EXPERTS // N_EXPERT_GROUPS
    logits = _dot_nt(wr_ref[...], x_ref[...], precision=HIGHEST)
    s = _sigmoid(logits)
    sb = s + rb_ref[...]
    sub = lax.broadcasted_iota(jnp.int32, (per_grp, tm), 0).astype(F32)
    slabs, gscore = [], []
    for gi in range(N_EXPERT_GROUPS):
        slab = sb[gi * per_grp:(gi + 1) * per_grp, :]
        m1 = jnp.max(slab, axis=0, keepdims=True)
        f1 = jnp.min(jnp.where(slab == m1, sub, float(per_grp)), axis=0, keepdims=True)
        m2 = jnp.max(jnp.where(sub == f1, -jnp.inf, slab), axis=0, keepdims=True)
        slabs.append(slab)
        gscore.append(m1 + m2)
    masked = []
    for gi in range(N_EXPERT_GROUPS):
        beaten = jnp.zeros((1, tm), F32)
        for gj in range(N_EXPERT_GROUPS):
            if gj < gi:
                beaten = beaten + jnp.where(gscore[gj] >= gscore[gi], 1.0, 0.0)
            elif gj > gi:
                beaten = beaten + jnp.where(gscore[gj] > gscore[gi], 1.0, 0.0)
        masked.append(jnp.where(beaten < TOPK_GROUPS - 0.5, slabs[gi], NEG_INF))
    esel = _top_k_mask(jnp.concatenate(masked, axis=0), MOE_TOPK, 0)
    w = s * esel
    w = w / jnp.sum(w, axis=0, keepdims=True) * ROUTED_SCALE
    wpad = jnp.concatenate([w, jnp.zeros((LANES - N_EXPERTS, tm), F32)], axis=0)
    gate_ref[...] = wpad.T


def _router(x, wr_t, rb, tm):
    m = x.shape[0]
    tm = min(tm, m)
    return pl.pallas_call(
        _router_kernel,
        out_shape=jax.ShapeDtypeStruct((m, LANES), F32),
        grid=(m // tm,),
        in_specs=[pl.BlockSpec((tm, D_MODEL), lambda i: (i, 0)),
                  pl.BlockSpec(wr_t.shape, lambda i: (0, 0)),
                  pl.BlockSpec(rb.shape, lambda i: (0, 0))],
        out_specs=pl.BlockSpec((tm, LANES), lambda i: (i, 0)),
        compiler_params=_params(("parallel",)),
        name="router",
    )(x, wr_t, rb)


def _moe_kernel(x_ref, gate_ref, wg_ref, wu_ref, wd_ref, sg_ref, su_ref, sd_ref, g_ref, b_ref,
                o_ref, xb_s):
    e = pl.program_id(1)

    @pl.when(e == 0)
    def _():
        xb0 = x_ref[...].astype(BF16)
        xb_s[...] = xb0
        hg = _dot(xb0, sg_ref[...])
        hs = hg * _sigmoid(hg) * _dot(xb0, su_ref[...])
        o_ref[...] = _dot(hs.astype(BF16), sd_ref[...])

    xb = xb_s[...]
    hg = _dot(xb, wg_ref[0])
    h = hg * _sigmoid(hg) * _dot(xb, wu_ref[0])
    gate = gate_ref[...]
    lane = lax.broadcasted_iota(jnp.int32, gate.shape, 1)
    gcol = jnp.sum(jnp.where(lane == e, gate, 0.0), axis=1, keepdims=True)
    o_ref[...] += _dot((h * gcol).astype(BF16), wd_ref[0])

    @pl.when(e == pl.num_programs(1) - 1)
    def _():
        o_ref[...] = _layer_norm(ALPHA * x_ref[...] + o_ref[...], g_ref[...], b_ref[...])


def _moe(x, gate, wg, wu, wd, sg, su, sd, g, b, tm):
    m = x.shape[0]
    tm = min(tm, m)
    row = lambda i, e: (i, 0)
    const = lambda i, e: (0, 0)
    return pl.pallas_call(
        _moe_kernel,
        out_shape=jax.ShapeDtypeStruct((m, D_MODEL), F32),
        grid=(m // tm, N_EXPERTS),
        in_specs=[pl.BlockSpec((tm, D_MODEL), row),
                  pl.BlockSpec((tm, LANES), row),
                  pl.BlockSpec((1, D_MODEL, D_EXPERT), lambda i, e: (e, 0, 0)),
                  pl.BlockSpec((1, D_MODEL, D_EXPERT), lambda i, e: (e, 0, 0)),
                  pl.BlockSpec((1, D_EXPERT, D_MODEL), lambda i, e: (e, 0, 0)),
                  pl.BlockSpec(sg.shape, const), pl.BlockSpec(su.shape, const),
                  pl.BlockSpec(sd.shape, const),
                  pl.BlockSpec((1, D_MODEL), const), pl.BlockSpec((1, D_MODEL), const)],
        out_specs=pl.BlockSpec((tm, D_MODEL), row),
        scratch_shapes=[pltpu.VMEM((tm, D_MODEL), BF16)],
        compiler_params=_params(("parallel", "arbitrary")),
        name="moe",
    )(x, gate, wg, wu, wd, sg, su, sd, g, b)


def _rope_cs(pos, half):
    inv = ROPE_THETA ** (-jnp.arange(half, dtype=F32) / half)
    ang = pos.astype(F32)[:, None] * inv[None, :]
    return jnp.cos(ang), jnp.sin(ang)


def _even_table(pos):
    n = pos.shape[0]
    c32, s32 = _rope_cs(pos, 32)
    c16, s16 = _rope_cs(pos, 16)
    one = lambda w: jnp.ones((n, w), F32)
    zero = lambda w: jnp.zeros((n, w), F32)
    cos_a = jnp.tile(c32, (1, 4))
    sin_a = jnp.tile(jnp.concatenate([-s32, s32], axis=1), (1, 2))
    cos_b = jnp.concatenate([one(64), c16, c16, one(32)], axis=1)
    sin_b = jnp.concatenate([zero(64), -s16, s16, zero(32)], axis=1)
    cos_c = jnp.concatenate([c16, c16, one(96)], axis=1)
    sin_c = jnp.concatenate([-s16, s16, zero(96)], axis=1)
    return jnp.concatenate([cos_a, sin_a, cos_b, sin_b, cos_c, sin_c], axis=1)


def _odd_table(pos):
    c64, s64 = _rope_cs(pos, 64)
    return jnp.concatenate([c64, c64, -s64, s64], axis=1)


_MOBA_HEAD_PERM = (0, 2, 1, 3, 4, 6, 5, 7)


def _even_weights(w_in, w_uq, w_uk, w_uv, w_out):
    hp = np.asarray(_MOBA_HEAD_PERM)
    qcols = (hp[:, None] * A_HEAD_DIM + np.arange(A_HEAD_DIM)[None, :]).reshape(-1)
    w_q = w_in[:, :512][:, qcols]
    w_in_p = jnp.concatenate(
        [w_q, w_in[:, 512:], jnp.zeros((D_MODEL, EVEN_W - w_in.shape[1]), F32)], axis=1).astype(BF16)
    uq = w_uq.reshape(MLA_Q_RANK, B_HEADS, MLA_D_NOPE + MLA_D_ROPE)
    w_uq_p = jnp.pad(uq, ((0, 0), (0, 0), (0, 32))).reshape(MLA_Q_RANK, B_HEADS * 128).astype(BF16)
    eye = jnp.eye(MLA_D_ROPE, dtype=F32)
    k_top = jnp.pad(w_uk, ((0, 0), (0, 0), (0, 64)))
    k_mid = jnp.broadcast_to(jnp.pad(eye, ((0, 0), (64, 32)))[:, None, :], (MLA_D_ROPE, B_HEADS, 128))
    k_all = jnp.concatenate([k_top, k_mid, jnp.zeros((96, B_HEADS, 128), F32)], axis=0).reshape(384, 1024)
    v_all = jnp.concatenate([w_uv.reshape(MLA_KV_RANK, 512), jnp.zeros((128, 512), F32)], axis=0)
    w_kv = jnp.concatenate([k_all, v_all], axis=1).astype(BF16)
    blocks = []
    for h in range(B_HEADS):
        blk = jnp.zeros((128, 384), F32)
        blk = blk.at[0:64, 0:256].set(w_uk[:, h, :].T)
        blk = blk.at[64:96, 256:288].set(eye)
        blocks.append(blk)
    w_abs = jax.scipy.linalg.block_diag(*blocks).astype(BF16)
    w_uv_bd = jax.scipy.linalg.block_diag(*[w_uv[:, h, :] for h in range(B_HEADS)]).astype(BF16)
    rows = (hp[:, None] * A_HEAD_DIM + np.arange(A_HEAD_DIM)[None, :]).reshape(-1)
    w_out_moba_p = w_out[:512][rows].astype(BF16)
    w_out_moba = w_out[:512].astype(BF16)
    w_out_mla = w_out[512:].astype(BF16)
    return w_in_p, w_uq_p, w_kv, w_abs, w_uv_bd, w_out_moba_p, w_out_moba, w_out_mla


def kernel(x_prompt, x_sample, mem_prompt, cache_moba_k, cache_moba_v, cache_mla, cache_swa0_kv, cache_swa1_kv, cache_swa2_kv, cache_mem_kv, page_table, ln_gain, ln_bias, w_in_even, w_out_even, mla_q_norm, mla_w_uq, mla_kv_norm, mla_w_uk, mla_w_uv, w_in_odd, w_out_odd, w_mem_q, w_mem_kv, w_mem_o, w_router, router_bias, w_exp_gate, w_exp_up, w_exp_down, w_sh_gate, w_sh_up, w_sh_down):
    bp, sp, _ = x_prompt.shape
    bs, ss, _ = x_sample.shape
    n_p = bp * sp
    n_s = bs * ss
    tm = 256
    n_pool = cache_moba_k.shape[1]
    swa_caches = (cache_swa0_kv, cache_swa1_kv, cache_swa2_kv)

    xp = x_prompt.reshape(n_p, D_MODEL)
    xs = x_sample.reshape(n_s, D_MODEL)
    pos_p = jnp.arange(sp)
    pos_s = jnp.tile(PAST_LEN + jnp.arange(ss), n_s // ss)
    tab_even_p, tab_even_s = _even_table(pos_p), _even_table(pos_s)
    tab_odd_p, tab_odd_s = _odd_table(pos_p), _odd_table(pos_s)
    mem2 = mem_prompt.reshape(bp * N_MEM, D_MODEL)

    ck_pages = jnp.transpose(cache_moba_k, (0, 1, 3, 4, 2)).reshape(-1, A_KV_HEADS * A_HEAD_DIM, PAGE_SIZE)
    cv_pages = jnp.transpose(cache_moba_v, (0, 1, 3, 4, 2)).reshape(-1, A_KV_HEADS * A_HEAD_DIM, PAGE_SIZE)
    cm_pages = jnp.transpose(cache_mla, (0, 1, 3, 2)).reshape(-1, MLA_ROW, PAGE_SIZE)
    swa_rows = [c.reshape(-1, SWA_KEYS, dil * 2 * C_HEADS, C_HEAD_DIM)
                for c, (win, dil) in zip(swa_caches, SWA_CONFIGS)]
    mem_rows = cache_mem_kv.reshape(-1, N_MEM * 2 * MEM_HEADS, MEM_HEAD_DIM)

    moba_k_p, moba_v_p, mla_p, mem_p = [], [], [], []
    moba_k_s, moba_v_s, mla_s = [], [], []
    swa_p = [[] for _ in range(N_SWA)]
    swa_s = [[] for _ in range(N_SWA)]

    for l in range(DEPTH):
        ln_g = [ln_gain[l, i].reshape(1, D_MODEL) for i in range(3)]
        ln_b = [ln_bias[l, i].reshape(1, D_MODEL) for i in range(3)]
        if l % 2 == 0:
            e = l // 2
            (w_in_p, w_uq_p, w_kv, w_abs, w_uv_bd, w_out_moba_p, w_out_moba, w_out_mla) = _even_weights(
                w_in_even[e], mla_w_uq[e], mla_w_uk[e], mla_w_uv[e], w_out_even[e])
            qn = mla_q_norm[e].reshape(1, MLA_Q_RANK)
            kvn = mla_kv_norm[e].reshape(1, MLA_KV_RANK)
            qa, ka, va, km, mla, qm, kx, vm = _even_proj(
                xp, tab_even_p, sp // tm, w_in_p, qn, w_uq_p, kvn, w_kv, tm)
            o_moba = _moba_prompt(qa, ka, va, km.reshape(n_p // MOBA_BLOCK, 256), bp, sp)
            o_mla = _mla_prompt(qm, kx, vm, bp, sp)
            xp = _mm_res_ln([o_moba, o_mla], [w_out_moba_p, w_out_mla], xp, ln_g[0], ln_b[0], tm)
            moba_k_p.append(ka.reshape(bp, sp, A_KV_HEADS, A_HEAD_DIM))
            moba_v_p.append(va.reshape(bp, sp, A_KV_HEADS, A_HEAD_DIM))
            mla_p.append(mla.reshape(bp, sp, MLA_ROW))
            qa, ka, va, km, mla, qm, kx, vm = _even_proj(
                xs, tab_even_s, n_s // tm, w_in_p, qn, w_uq_p, kvn, w_kv, tm)
            inv = np.argsort(np.asarray(_MOBA_HEAD_PERM))
            q5 = qa.reshape(bs, ss, A_HEADS, A_HEAD_DIM)[:, :, inv].reshape(bs, ss, A_KV_HEADS, 2, A_HEAD_DIM)
            eye_kv = jnp.eye(A_KV_HEADS, dtype=F32)[None, :, None, None, :, None]
            q_rows = (q5.transpose(0, 2, 1, 3, 4)[:, :, :, :, None, :] * eye_kv).reshape(
                bs, A_KV_HEADS * ss * 2, A_KV_HEADS * A_HEAD_DIM)
            o_ms = _moba_sample(page_table, q_rows, ka.reshape(bs, ss, 256), va.reshape(bs, ss, 256),
                                ck_pages, cv_pages, e * n_pool)
            o_ms = o_ms.reshape(bs, ss, 2, A_KV_HEADS, A_HEAD_DIM).transpose(0, 1, 3, 2, 4).reshape(n_s, 512)
            q_abs = _matmul(qm, w_abs, tm, BF16).reshape(bs, ss * B_HEADS, 384)
            o_lat = _mla_sample(page_table, q_abs, mla.reshape(bs, ss, MLA_ROW), cm_pages, e * n_pool)
            o_mls = _matmul(o_lat.reshape(n_s, B_HEADS * MLA_KV_RANK), w_uv_bd, tm)
            xs = _mm_res_ln([o_ms, o_mls], [w_out_moba, w_out_mla], xs, ln_g[0], ln_b[0], tm)
            moba_k_s.append(ka.reshape(bs, ss, A_KV_HEADS, A_HEAD_DIM))
            moba_v_s.append(va.reshape(bs, ss, A_KV_HEADS, A_HEAD_DIM))
            mla_s.append(mla.reshape(bs, ss, MLA_ROW))
        else:
            o = l // 2
            w_odd = w_in_odd[o].astype(BF16)
            w_oo = w_out_odd[o].astype(BF16)
            h_p = _odd_proj(xp, tab_odd_p, sp // tm, w_odd, tm)
            h3 = h_p.reshape(bp, sp, ODD_W)
            outs, lses = [], []
            for g, (win, dil) in enumerate(SWA_CONFIGS):
                og, lg = _dilated_prompt(h3, g, dil)
                outs.append(og)
                lses.append(lg)
                keep = min(win, sp)
                swa_p[g].append(h3[:, sp - keep:, g * 1536 + 512:(g + 1) * 1536].reshape(
                    bp, keep, 2, C_HEADS, C_HEAD_DIM))
            xp = _odd_out(outs, lses, w_oo, xp, ln_g[0], ln_b[0], tm)
            h_s = _odd_proj(xs, tab_odd_s, n_s // tm, w_odd, tm)
            q_list = [h_s[:, g * 1536:g * 1536 + 512].reshape(bs, ss * C_HEADS, C_HEAD_DIM)
                      for g in range(N_SWA)]
            kv_new = [h_s[:, g * 1536 + 512:(g + 1) * 1536].reshape(bs, ss * 2 * C_HEADS, C_HEAD_DIM)
                      for g in range(N_SWA)]
            new_list = [jnp.pad(a, ((0, 0), (0, LANES - a.shape[1]), (0, 0))) for a in kv_new]
            outs_s, lses_s = _dilated_sample(q_list, new_list, swa_rows, o * bs)
            lses_s = [jnp.pad(a[:, :, 0].reshape(n_s, C_HEADS), ((0, 0), (0, LANES - C_HEADS))) for a in lses_s]
            xs = _odd_out([a.reshape(n_s, 512) for a in outs_s], lses_s, w_oo, xs, ln_g[0], ln_b[0], tm)
            for g in range(N_SWA):
                swa_s[g].append(kv_new[g].reshape(bs, ss, 2, C_HEADS, C_HEAD_DIM))
        w_mq = w_mem_q[l].astype(BF16)
        w_mo = w_mem_o[l].astype(BF16)
        mkv = _matmul(mem2, w_mem_kv[l].astype(BF16), tm)
        mem_p.append(mkv.reshape(bp, N_MEM, 2, MEM_HEADS, MEM_HEAD_DIM))
        xp = _mem_prompt(xp, mkv.reshape(bp, N_MEM, 1024).astype(BF16), w_mq, w_mo,
                         ln_g[1], ln_b[1], sp // tm, tm)
        q_s = _matmul(xs, w_mq, tm).reshape(bs, ss * MEM_HEADS, MEM_HEAD_DIM)
        att_s = _mem_sample_attn(q_s, mem_rows, l * bs)
        xs = _mm_res_ln([att_s.reshape(n_s, 512)], [w_mo], xs, ln_g[1], ln_b[1], tm)
        wr_t = w_router[l].T
        rb = router_bias[l].reshape(N_EXPERTS, 1)
        wg, wu, wd = w_exp_gate[l].astype(BF16), w_exp_up[l].astype(BF16), w_exp_down[l].astype(BF16)
        sg, su, sd = w_sh_gate[l].astype(BF16), w_sh_up[l].astype(BF16), w_sh_down[l].astype(BF16)
        gate_p = _router(xp, wr_t, rb, 512)
        xp = _moe(xp, gate_p, wg, wu, wd, sg, su, sd, ln_g[2], ln_b[2], 1024)
        gate_s = _router(xs, wr_t, rb, 512)
        xs = _moe(xs, gate_s, wg, wu, wd, sg, su, sd, ln_g[2], ln_b[2], 512)

    return (xp.reshape(bp, sp, D_MODEL), xs.reshape(bs, ss, D_MODEL),
            jnp.stack(moba_k_p), jnp.stack(moba_v_p), jnp.stack(mla_p),
            jnp.stack(swa_p[0]), jnp.stack(swa_p[1]), jnp.stack(swa_p[2]), jnp.stack(mem_p),
            jnp.stack(moba_k_s), jnp.stack(moba_v_s), jnp.stack(mla_s),
            jnp.stack(swa_s[0]), jnp.stack(swa_s[1]), jnp.stack(swa_s[2]))
```

```python
import functools

import numpy as np
import jax
import jax.numpy as jnp
from jax import lax
from jax.experimental import pallas as pl
from jax.experimental.pallas import tpu as pltpu

F32 = jnp.float32
BF16 = jnp.bfloat16
HIGHEST = lax.Precision.HIGHEST

D_MODEL = 1024
DEPTH = 4
PAST_LEN = 8192
PAGE_SIZE = 128
A_HEADS = 8
A_KV_HEADS = 4
A_HEAD_DIM = 64
MOBA_BLOCK = 256
MOBA_TOPK = 3
B_HEADS = 8
MLA_Q_RANK = 384
MLA_KV_RANK = 256
MLA_D_NOPE = 64
MLA_D_ROPE = 32
MLA_D_V = 64
MLA_ROW = MLA_KV_RANK + MLA_D_ROPE
SWA_CONFIGS = ((128, 1), (512, 4), (2048, 16))
N_SWA = 3
C_HEADS = 4
C_HEAD_DIM = 128
SWA_KEYS = 128
MLA_KV_TILE = 256
N_MEM = 256
MEM_HEADS = 4
MEM_HEAD_DIM = 128
N_EXPERTS = 64
N_EXPERT_GROUPS = 8
TOPK_GROUPS = 4
MOE_TOPK = 8
D_EXPERT = 256
ROUTED_SCALE = 2.5
ROPE_THETA = 10000.0
LN_EPS = 1e-5
RMS_EPS = 1e-6
NEG_INF = -1e30
ALPHA = (2 * DEPTH) ** 0.25
A_SCALE = A_HEAD_DIM ** -0.5
MLA_SCALE = (MLA_D_NOPE + MLA_D_ROPE) ** -0.5
C_SCALE = C_HEAD_DIM ** -0.5
MEM_SCALE = MEM_HEAD_DIM ** -0.5

LANES = 128
EVEN_W = 1792
ODD_W = N_SWA * 3 * C_HEADS * C_HEAD_DIM
VMEM_LIMIT = 56 * 1024 * 1024

_NT = (((1,), (1,)), ((), ()))


def _params(sem, vmem=VMEM_LIMIT):
    return pltpu.CompilerParams(dimension_semantics=sem, vmem_limit_bytes=vmem)


def _dot(a, b):
    return jnp.dot(a, b, preferred_element_type=F32)


def _dot_nt(a, b, precision=None):
    return lax.dot_general(a, b, _NT, precision=precision, preferred_element_type=F32)


def _layer_norm(z, g, b):
    mu = jnp.mean(z, axis=-1, keepdims=True)
    zc = z - mu
    var = jnp.mean(zc * zc, axis=-1, keepdims=True)
    return zc * lax.rsqrt(var + LN_EPS) * g + b


def _rms(x, g):
    return x * lax.rsqrt(jnp.mean(x * x, axis=-1, keepdims=True) + RMS_EPS) * g


def _sigmoid(x):
    return 1.0 / (1.0 + jnp.exp(-x))


def _top_k_mask(vals, k, axis):
    n = vals.shape[axis]
    idx = lax.broadcasted_iota(jnp.int32, vals.shape, axis).astype(F32)
    sel = jnp.zeros(vals.shape, F32)
    for _ in range(k):
        mx = jnp.max(vals, axis=axis, keepdims=True)
        first = jnp.min(jnp.where(vals == mx, idx, float(n)), axis=axis, keepdims=True)
        hit = idx == first
        sel = jnp.where(hit, 1.0, sel)
        vals = jnp.where(hit, -jnp.inf, vals)
    return sel


def _rope_block(yb, cos, sin, half, lane):
    fwd = pltpu.roll(yb, LANES - half, 1)
    bwd = pltpu.roll(yb, half, 1)
    first = (lane & (2 * half - 1)) < half
    return yb * cos + jnp.where(first, fwd, bwd) * sin


def _mm_kernel(x_ref, w_ref, o_ref):
    o_ref[...] = _dot(x_ref[...].astype(BF16), w_ref[...]).astype(o_ref.dtype)


def _matmul(x, w, tm, out_dtype=F32):
    m, k = x.shape
    n = w.shape[1]
    return pl.pallas_call(
        _mm_kernel,
        out_shape=jax.ShapeDtypeStruct((m, n), out_dtype),
        grid=(m // tm,),
        in_specs=[pl.BlockSpec((tm, k), lambda i: (i, 0)),
                  pl.BlockSpec((k, n), lambda i: (0, 0))],
        out_specs=pl.BlockSpec((tm, n), lambda i: (i, 0)),
        compiler_params=_params(("parallel",)),
        name="matmul",
    )(x, w)


def _mm_res_ln_kernel(n_in, *refs):
    a_refs = refs[:n_in]
    w_refs = refs[n_in:2 * n_in]
    r_ref, g_ref, b_ref, o_ref = refs[2 * n_in:]
    acc = None
    for a_ref, w_ref in zip(a_refs, w_refs):
        t = _dot(a_ref[...].astype(BF16), w_ref[...])
        acc = t if acc is None else acc + t
    o_ref[...] = _layer_norm(ALPHA * r_ref[...] + acc, g_ref[...], b_ref[...])


def _mm_res_ln(a_list, w_list, resid, g, b, tm):
    m = resid.shape[0]
    n_in = len(a_list)
    in_specs = [pl.BlockSpec((tm, a.shape[1]), lambda i: (i, 0)) for a in a_list]
    in_specs += [pl.BlockSpec(w.shape, lambda i: (0, 0)) for w in w_list]
    in_specs += [pl.BlockSpec((tm, D_MODEL), lambda i: (i, 0)),
                 pl.BlockSpec((1, D_MODEL), lambda i: (0, 0)),
                 pl.BlockSpec((1, D_MODEL), lambda i: (0, 0))]
    return pl.pallas_call(
        functools.partial(_mm_res_ln_kernel, n_in),
        out_shape=jax.ShapeDtypeStruct((m, D_MODEL), F32),
        grid=(m // tm,),
        in_specs=in_specs,
        out_specs=pl.BlockSpec((tm, D_MODEL), lambda i: (i, 0)),
        compiler_params=_params(("parallel",)),
        name="mm_res_ln",
    )(*a_list, *w_list, resid, g, b)


def _even_proj_kernel(x_ref, tab_ref, win_ref, qn_ref, wuq_ref, kvn_ref, wkv_ref,
                      qa_ref, ka_ref, va_ref, km_ref, mla_ref, qm_ref, kx_ref, vat_ref, vmt_ref):
    tm = x_ref.shape[0]
    lane = lax.broadcasted_iota(jnp.int32, (tm, LANES), 1)
    y = _dot(x_ref[...].astype(BF16), win_ref[...])
    cos_a, sin_a = tab_ref[:, 0:128], tab_ref[:, 128:256]
    cos_b, sin_b = tab_ref[:, 256:384], tab_ref[:, 384:512]
    cos_c, sin_c = tab_ref[:, 512:640], tab_ref[:, 640:768]
    for j in range(4):
        qa_ref[:, j * 128:(j + 1) * 128] = _rope_block(
            y[:, j * 128:(j + 1) * 128], cos_a, sin_a, 32, lane)
    for j in range(2):
        kb = _rope_block(y[:, 512 + j * 128:640 + j * 128], cos_a, sin_a, 32, lane)
        ka_ref[:, j * 128:(j + 1) * 128] = kb
        km_ref[0, :, j * 128:(j + 1) * 128] = jnp.mean(kb, axis=0, keepdims=True)
    va_ref[...] = y[:, 768:1024]
    vat_ref[...] = y[:, 768:1024].T.astype(BF16)
    cq = _rms(y[:, 1024:1408], qn_ref[...])
    qm = _dot(cq.astype(BF16), wuq_ref[...])
    for j in range(B_HEADS):
        qj = _rope_block(qm[:, j * 128:(j + 1) * 128], cos_b, sin_b, 16, lane)
        qm_ref[:, j * 128:(j + 1) * 128] = (qj * MLA_SCALE).astype(BF16)
    ckv = _rms(y[:, 1408:1664], kvn_ref[...])
    kpe = _rope_block(y[:, 1664:1792], cos_c, sin_c, 16, lane)
    mla_ref[:, 0:MLA_KV_RANK] = ckv
    mla_ref[:, MLA_KV_RANK:MLA_ROW] = kpe[:, 0:MLA_D_ROPE]
    ck = jnp.concatenate([ckv, kpe], axis=1).astype(BF16)
    kv = _dot(ck, wkv_ref[...])
    kx_ref[...] = kv[:, 0:1024].astype(BF16)
    vmt_ref[...] = kv[:, 1024:1536].T.astype(BF16)


def _even_proj(x, tab, tab_blocks, w_in, q_norm, w_uq, kv_norm, w_kv, tm):
    m = x.shape[0]
    nt = m // tm
    row = lambda i: (i, 0)
    const = lambda i: (0, 0)
    outs = (
        jax.ShapeDtypeStruct((m, 512), F32),
        jax.ShapeDtypeStruct((m, 256), F32),
        jax.ShapeDtypeStruct((m, 256), F32),
        jax.ShapeDtypeStruct((nt, 1, 256), F32),
        jax.ShapeDtypeStruct((m, MLA_ROW), F32),
        jax.ShapeDtypeStruct((m, 1024), BF16),
        jax.ShapeDtypeStruct((m, 1024), BF16),
        jax.ShapeDtypeStruct((256, m), BF16),
        jax.ShapeDtypeStruct((512, m), BF16),
    )
    return pl.pallas_call(
        _even_proj_kernel,
        out_shape=outs,
        grid=(nt,),
        in_specs=[pl.BlockSpec((tm, D_MODEL), row),
                  pl.BlockSpec((tm, 768), lambda i: (i % tab_blocks, 0)),
                  pl.BlockSpec(w_in.shape, const),
                  pl.BlockSpec(q_norm.shape, const),
                  pl.BlockSpec(w_uq.shape, const),
                  pl.BlockSpec(kv_norm.shape, const),
                  pl.BlockSpec(w_kv.shape, const)],
        out_specs=(pl.BlockSpec((tm, 512), row),
                   pl.BlockSpec((tm, 256), row),
                   pl.BlockSpec((tm, 256), row),
                   pl.BlockSpec((1, 1, 256), lambda i: (i, 0, 0)),
                   pl.BlockSpec((tm, MLA_ROW), row),
                   pl.BlockSpec((tm, 1024), row),
                   pl.BlockSpec((tm, 1024), row),
                   pl.BlockSpec((256, tm), lambda i: (0, i)),
                   pl.BlockSpec((512, tm), lambda i: (0, i))),
        compiler_params=_params(("parallel",)),
        name="even_proj",
    )(x, tab, w_in, q_norm, w_uq, kv_norm, w_kv)


_TN = (((0,), (0,)), ((), ()))


def _dot_tn(a, b):
    return lax.dot_general(a, b, _TN, preferred_element_type=F32)


def _moba_p_kernel(q_ref, k_ref, v_ref, km_ref, o_ref, sel_s, m_s, l_s, acc_s):
    tq = MOBA_BLOCK
    c = pl.program_id(2)
    n_blk = km_ref.shape[0]
    lane = lax.broadcasted_iota(jnp.int32, (tq, LANES), 1)
    lo = lane < A_HEAD_DIM
    q_a = q_ref[:, 0:128]
    q_b = q_ref[:, 128:256]
    q4 = jnp.concatenate([jnp.where(lo, q_a, 0.0), jnp.where(lo, 0.0, q_a),
                          jnp.where(lo, q_b, 0.0), jnp.where(lo, 0.0, q_b)], axis=0)
    gate = _dot_nt(km_ref[...], q4, precision=HIGHEST)
    past = lax.broadcasted_iota(jnp.int32, gate.shape, 0) < c
    sel = _top_k_mask(jnp.where(past, gate, NEG_INF), min(MOBA_TOPK, n_blk), 0)
    sel_s[...] = jnp.where(past, sel, 0.0)

    qb = (q4 * A_SCALE).astype(BF16)

    vrow = lax.broadcasted_iota(jnp.int32, (LANES, tq), 0)
    one_hi = jnp.where(vrow == A_HEAD_DIM, 1.0, 0.0).astype(BF16)
    one_lo = jnp.where(vrow == 0, 1.0, 0.0).astype(BF16)

    def kv_tile(j):
        start = pl.multiple_of(j * tq, tq)
        v = v_ref[:, pl.ds(start, tq)]
        return (k_ref[pl.ds(start, tq), :].astype(BF16),
                jnp.where(vrow < A_HEAD_DIM, v, one_hi), jnp.where(vrow < A_HEAD_DIM, one_lo, v))

    def weighted(p, v_lo, v_hi):
        r = [_dot(v_lo if i % 2 == 0 else v_hi, p[:, i * tq:(i + 1) * tq]) for i in range(4)]
        sums = [r[i][A_HEAD_DIM:A_HEAD_DIM + 1] if i % 2 == 0 else r[i][0:1] for i in range(4)]
        return jnp.concatenate(sums, axis=1), jnp.concatenate(r, axis=1)

    k_own, v_lo, v_hi = kv_tile(c)
    s = _dot_nt(k_own, qb)
    key = lax.broadcasted_iota(jnp.int32, s.shape, 0)
    qpos = lax.broadcasted_iota(jnp.int32, s.shape, 1) & (tq - 1)
    s = jnp.where(key <= qpos, s, NEG_INF)
    m0 = jnp.max(s, axis=0, keepdims=True)
    m_s[...] = m0
    l_s[...], acc_s[...] = weighted(jnp.exp((s - m0).astype(BF16)), v_lo, v_hi)

    def body(j, carry):
        kj, vj_lo, vj_hi = kv_tile(j)
        chosen = sel_s[pl.ds(j, 1), :] > 0.5
        sj = jnp.where(chosen, _dot_nt(kj, qb), NEG_INF)
        m_old = m_s[...]
        m_new = jnp.maximum(m_old, jnp.max(sj, axis=0, keepdims=True))
        a = jnp.exp(m_old - m_new)
        sums, vals = weighted(jnp.exp((sj - m_new).astype(BF16)), vj_lo, vj_hi)
        m_s[...] = m_new
        l_s[...] = a * l_s[...] + sums
        acc_s[...] = a * acc_s[...] + vals
        return carry

    lax.fori_loop(0, c, body, 0)
    o = acc_s[...] / l_s[...]
    top = lax.broadcasted_iota(jnp.int32, (LANES, tq), 0) < A_HEAD_DIM
    o_ref[:, 0:128] = jnp.where(top, o[:, 0:tq], o[:, tq:2 * tq]).T
    o_ref[:, 128:256] = jnp.where(top, o[:, 2 * tq:3 * tq], o[:, 3 * tq:4 * tq]).T


def _moba_prompt(qa, ka, va, kmean, bsz, seq):
    tq = MOBA_BLOCK
    nc = seq // tq
    return pl.pallas_call(
        _moba_p_kernel,
        out_shape=jax.ShapeDtypeStruct((bsz * seq, 512), F32),
        grid=(bsz, 2, nc),
        in_specs=[pl.BlockSpec((tq, 256), lambda b, p, c: (b * nc + c, p)),
                  pl.BlockSpec((seq, 128), lambda b, p, c: (b, p)),
                  pl.BlockSpec((128, seq), lambda b, p, c: (p, b)),
                  pl.BlockSpec((nc, 128), lambda b, p, c: (b, p))],
        out_specs=pl.BlockSpec((tq, 256), lambda b, p, c: (b * nc + c, p)),
        scratch_shapes=[pltpu.VMEM((nc, 4 * tq), F32), pltpu.VMEM((1, 4 * tq), F32),
                        pltpu.VMEM((1, 4 * tq), F32), pltpu.VMEM((128, 4 * tq), F32)],
        compiler_params=_params(("parallel", "parallel", "arbitrary")),
        name="moba_prompt",
    )(qa, ka, va, kmean)


def _mla_p_kernel(q_ref, k_ref, v_ref, o_ref):
    tq = q_ref.shape[0]
    tk = MLA_KV_TILE
    per_q = tq // tk
    c = pl.program_id(2)
    q0 = q_ref[:, 0:128]
    q1 = q_ref[:, 128:256]
    top = lax.broadcasted_iota(jnp.int32, (LANES, tq), 0) < MLA_D_V
    vrow = lax.broadcasted_iota(jnp.int32, (LANES, tk), 0)
    one_hi = jnp.where(vrow == MLA_D_V, 1.0, 0.0).astype(BF16)
    one_lo = jnp.where(vrow == 0, 1.0, 0.0).astype(BF16)

    def scores(j):
        k = k_ref[pl.ds(pl.multiple_of(j * tk, tk), tk), :]
        return _dot_nt(k[:, 0:128], q0), _dot_nt(k[:, 128:256], q1)

    def consume(j, s0, s1, diag_offset, state):
        m0, m1, l0, l1, acc = state
        v = v_ref[:, pl.ds(pl.multiple_of(j * tk, tk), tk)]
        if diag_offset is not None:
            key = lax.broadcasted_iota(jnp.int32, s0.shape, 0) + diag_offset
            qpos = lax.broadcasted_iota(jnp.int32, s0.shape, 1)
            s0 = jnp.where(key <= qpos, s0, NEG_INF)
            s1 = jnp.where(key <= qpos, s1, NEG_INF)
        m0n = jnp.maximum(m0, jnp.max(s0, axis=0, keepdims=True))
        m1n = jnp.maximum(m1, jnp.max(s1, axis=0, keepdims=True))
        a0 = jnp.exp(m0 - m0n)
        a1 = jnp.exp(m1 - m1n)
        r0 = _dot(jnp.where(vrow < MLA_D_V, v, one_hi), jnp.exp((s0 - m0n).astype(BF16)))
        r1 = _dot(jnp.where(vrow < MLA_D_V, one_lo, v), jnp.exp((s1 - m1n).astype(BF16)))
        l0 = a0 * l0 + r0[MLA_D_V:MLA_D_V + 1]
        l1 = a1 * l1 + r1[0:1]
        acc = jnp.where(top, a0, a1) * acc + jnp.where(top, r0, r1)
        return m0n, m1n, l0, l1, acc

    neg = jnp.full((1, tq), NEG_INF, F32)
    zero = jnp.zeros((1, tq), F32)
    state = (neg, neg, zero, zero, jnp.zeros((LANES, tq), F32))
    for d in range(per_q):
        state = consume(c * per_q + d, *scores(c * per_q + d), d * tk, state)

    n_past = c * per_q

    def body(j, carry):
        s0, s1, st = carry
        nxt = scores(jnp.minimum(j + 1, n_past - 1))
        return nxt + (consume(j, s0, s1, None, st),)

    _, _, (m0, m1, l0, l1, acc) = lax.fori_loop(0, n_past, body, scores(0) + (state,))
    o_ref[...] = (acc / jnp.where(top, l0, l1)).T


def _mla_prompt(qm, kx, vm, bsz, seq, tq=512):
    nc = seq // tq
    return pl.pallas_call(
        _mla_p_kernel,
        out_shape=jax.ShapeDtypeStruct((bsz * seq, 512), F32),
        grid=(bsz, B_HEADS // 2, nc),
        in_specs=[pl.BlockSpec((tq, 256), lambda b, h, c: (b * nc + c, h)),
                  pl.BlockSpec((seq, 256), lambda b, h, c: (b, h)),
                  pl.BlockSpec((128, seq), lambda b, h, c: (h, b))],
        out_specs=pl.BlockSpec((tq, 128), lambda b, h, c: (b * nc + c, h)),
        compiler_params=_params(("parallel", "parallel", "arbitrary")),
        name="mla_prompt",
    )(qm, kx, vm)


def _page_copies(pt_ref, cache_hbm, buf, sem, bb, slot, n_pages, page_base):
    copies = []
    for j in range(n_pages):
        pg = pt_ref[bb, j] + page_base
        copies.append(pltpu.make_async_copy(
            cache_hbm.at[pg], buf.at[slot, :, pl.ds(j * PAGE_SIZE, PAGE_SIZE)], sem.at[slot]))
    return copies


def _prefetch_pages(pt_ref, caches, bufs, sems, n_pages, page_base):
    b = pl.program_id(0)
    nb = pl.num_programs(0)
    slot = b % 2

    def start(bb, sl):
        for cache, buf, sem in zip(caches, bufs, sems):
            for cp in _page_copies(pt_ref, cache, buf, sem, bb, sl, n_pages, page_base):
                cp.start()

    @pl.when(b == 0)
    def _():
        start(0, 0)

    @pl.when(b + 1 < nb)
    def _():
        start(b + 1, 1 - slot)

    for cache, buf, sem in zip(caches, bufs, sems):
        for cp in _page_copies(pt_ref, cache, buf, sem, b, slot, n_pages, page_base):
            cp.wait()
    return slot


def _moba_s_kernel(page_base, pt_ref, q_ref, kn_ref, vn_ref, kc_hbm, vc_hbm, o_ref,
                   kbuf, vbuf, s_s, ksem, vsem):
    n_keys = kbuf.shape[2]
    n_pages = n_keys // PAGE_SIZE
    n_blk = n_keys // MOBA_BLOCK
    chunk = 1024
    n_chunk = n_keys // chunk
    slot = _prefetch_pages(pt_ref, (kc_hbm, vc_hbm), (kbuf, vbuf), (ksem, vsem), n_pages, page_base)

    q = q_ref[0]
    qb = (q * A_SCALE).astype(BF16)
    feat = kbuf.shape[1]
    blk_lane = lax.broadcasted_iota(jnp.int32, (feat, LANES), 1)
    km_t = jnp.zeros((feat, LANES), F32)
    for n in range(n_blk):
        mean_n = jnp.mean(kbuf[slot, :, n * MOBA_BLOCK:(n + 1) * MOBA_BLOCK], axis=1, keepdims=True)
        km_t = jnp.where(blk_lane == n, mean_n, km_t)
    for ci in range(n_chunk):
        kc = kbuf[slot, :, ci * chunk:(ci + 1) * chunk]
        s_s[:, ci * chunk:(ci + 1) * chunk] = _dot(qb, kc.astype(BF16))
    gate = jnp.dot(q, km_t, precision=HIGHEST, preferred_element_type=F32)[:, 0:n_blk]
    sel = _top_k_mask(gate, min(MOBA_TOPK, n_blk), 1)
    key_blk = lax.broadcasted_iota(jnp.int32, (n_blk, n_keys), 1) >> 8
    expand = jnp.where(key_blk == lax.broadcasted_iota(jnp.int32, (n_blk, n_keys), 0), 1.0, 0.0)
    chosen = _dot(sel.astype(BF16), expand.astype(BF16)) > 0.5
    s = jnp.where(chosen, s_s[...], NEG_INF)

    s_new = _dot_nt(qb, kn_ref[0].astype(BF16))
    tok = (lax.broadcasted_iota(jnp.int32, s_new.shape, 0) & 7) >> 1
    s_new = jnp.where(lax.broadcasted_iota(jnp.int32, s_new.shape, 1) <= tok, s_new, NEG_INF)
    m = jnp.maximum(jnp.max(s, axis=1, keepdims=True), jnp.max(s_new, axis=1, keepdims=True))
    p = jnp.exp(s - m)
    p_new = jnp.exp(s_new - m)
    l = jnp.sum(p, axis=1, keepdims=True) + jnp.sum(p_new, axis=1, keepdims=True)
    acc = _dot(p_new.astype(BF16), vn_ref[0].astype(BF16))
    pb = p.astype(BF16)
    for ci in range(n_chunk):
        vc = vbuf[slot, :, ci * chunk:(ci + 1) * chunk].astype(BF16)
        acc = acc + _dot_nt(pb[:, ci * chunk:(ci + 1) * chunk], vc)
    o = acc / l
    lane_head = lax.broadcasted_iota(jnp.int32, (8, o.shape[1]), 1) >> 6
    out = jnp.zeros((8, o.shape[1]), F32)
    for k in range(A_KV_HEADS):
        out = out + jnp.where(lane_head == k, o[8 * k:8 * k + 8], 0.0)
    o_ref[0] = out


def _moba_sample(page_table, q_rows, k_new, v_new, cache_k, cache_v, page_base):
    dbs, n_pages = page_table.shape
    n_keys = n_pages * PAGE_SIZE
    grid_spec = pltpu.PrefetchScalarGridSpec(
        num_scalar_prefetch=1,
        grid=(dbs,),
        in_specs=[pl.BlockSpec((1,) + q_rows.shape[1:], lambda b, pt: (b, 0, 0)),
                  pl.BlockSpec((1,) + k_new.shape[1:], lambda b, pt: (b, 0, 0)),
                  pl.BlockSpec((1,) + v_new.shape[1:], lambda b, pt: (b, 0, 0)),
                  pl.BlockSpec(memory_space=pl.ANY),
                  pl.BlockSpec(memory_space=pl.ANY)],
        out_specs=pl.BlockSpec((1, 8, 256), lambda b, pt: (b, 0, 0)),
        scratch_shapes=[pltpu.VMEM((2, 256, n_keys), F32), pltpu.VMEM((2, 256, n_keys), F32),
                        pltpu.VMEM((q_rows.shape[1], n_keys), F32),
                        pltpu.SemaphoreType.DMA((2,)), pltpu.SemaphoreType.DMA((2,))],
    )
    return pl.pallas_call(
        functools.partial(_moba_s_kernel, page_base),
        out_shape=jax.ShapeDtypeStruct((dbs, 8, 256), F32),
        grid_spec=grid_spec,
        compiler_params=_params(("arbitrary",), 60 * 1024 * 1024),
        name="moba_sample",
    )(page_table, q_rows, k_new, v_new, cache_k, cache_v)


def _mla_s_kernel(page_base, pt_ref, q_ref, new_ref, c_hbm, o_ref, cbuf, s_s, csem):
    n_keys = cbuf.shape[2]
    n_pages = n_keys // PAGE_SIZE
    chunk = 1024
    n_chunk = n_keys // chunk
    slot = _prefetch_pages(pt_ref, (c_hbm,), (cbuf,), (csem,), n_pages, page_base)

    q = q_ref[0][:, 0:MLA_ROW]
    for ci in range(n_chunk):
        pc = cbuf[slot, :, ci * chunk:(ci + 1) * chunk].astype(BF16)
        s_s[:, ci * chunk:(ci + 1) * chunk] = _dot(q, pc)
    s = s_s[...]
    new = new_ref[0]
    s_new = _dot_nt(q, new.astype(BF16))
    tok = lax.broadcasted_iota(jnp.int32, s_new.shape, 0) >> 3
    s_new = jnp.where(lax.broadcasted_iota(jnp.int32, s_new.shape, 1) <= tok, s_new, NEG_INF)
    m = jnp.maximum(jnp.max(s, axis=1, keepdims=True), jnp.max(s_new, axis=1, keepdims=True))
    p = jnp.exp(s - m)
    p_new = jnp.exp(s_new - m)
    l = jnp.sum(p, axis=1, keepdims=True) + jnp.sum(p_new, axis=1, keepdims=True)
    acc = _dot(p_new.astype(BF16), new[:, 0:MLA_KV_RANK].astype(BF16))
    pb = p.astype(BF16)
    for ci in range(n_chunk):
        cc = cbuf[slot, 0:MLA_KV_RANK, ci * chunk:(ci + 1) * chunk].astype(BF16)
        acc = acc + _dot_nt(pb[:, ci * chunk:(ci + 1) * chunk], cc)
    o_ref[0] = acc / l


def _mla_sample(page_table, q_rows, new_rows, cache, page_base):
    dbs, n_pages = page_table.shape
    n_keys = n_pages * PAGE_SIZE
    grid_spec = pltpu.PrefetchScalarGridSpec(
        num_scalar_prefetch=1,
        grid=(dbs,),
        in_specs=[pl.BlockSpec((1,) + q_rows.shape[1:], lambda b, pt: (b, 0, 0)),
                  pl.BlockSpec((1,) + new_rows.shape[1:], lambda b, pt: (b, 0, 0)),
                  pl.BlockSpec(memory_space=pl.ANY)],
        out_specs=pl.BlockSpec((1, q_rows.shape[1], MLA_KV_RANK), lambda b, pt: (b, 0, 0)),
        scratch_shapes=[pltpu.VMEM((2, MLA_ROW, n_keys), F32),
                        pltpu.VMEM((q_rows.shape[1], n_keys), F32),
                        pltpu.SemaphoreType.DMA((2,))],
    )
    return pl.pallas_call(
        functools.partial(_mla_s_kernel, page_base),
        out_shape=jax.ShapeDtypeStruct((dbs, q_rows.shape[1], MLA_KV_RANK), F32),
        grid_spec=grid_spec,
        compiler_params=_params(("arbitrary",), 60 * 1024 * 1024),
        name="mla_sample",
    )(page_table, q_rows, new_rows, cache)


def _odd_proj_kernel(dils, x_ref, tab_ref, w_ref, o0_ref, o1_ref, o2_ref, scr):
    tm = x_ref.shape[0]
    lane = lax.broadcasted_iota(jnp.int32, (tm, LANES), 1)
    cos, sin = tab_ref[:, 0:128], tab_ref[:, 128:256]
    xb = x_ref[...].astype(BF16)
    for g, (o_ref, d) in enumerate(zip((o0_ref, o1_ref, o2_ref), dils)):
        y = _dot(xb, w_ref[:, g * 1536:(g + 1) * 1536])
        for j in range(12):
            cols = slice(j * 128, (j + 1) * 128)
            blk = y[:, cols]
            if j < 8:
                blk = _rope_block(blk, cos, sin, 64, lane)
            if d == 1:
                o_ref[0, 0, :, cols] = blk
            else:
                scr[...] = blk
                for r in range(d):
                    o_ref[0, r, :, cols] = scr[pl.ds(r, tm // d, stride=d), :]


def _odd_proj(x, tab, tab_blocks, w, tm, bsz, dils):
    m = x.shape[0]
    tpb = m // bsz // tm
    outs = tuple(jax.ShapeDtypeStruct((bsz, d, m // bsz // d, 1536), F32) for d in dils)
    return pl.pallas_call(
        functools.partial(_odd_proj_kernel, dils),
        out_shape=outs,
        grid=(m // tm,),
        in_specs=[pl.BlockSpec((tm, D_MODEL), lambda i: (i, 0)),
                  pl.BlockSpec((tm, 256), lambda i: (i % tab_blocks, 0)),
                  pl.BlockSpec(w.shape, lambda i: (0, 0))],
        out_specs=tuple(pl.BlockSpec((1, d, tm // d, 1536), lambda i: (i // tpb, 0, i % tpb, 0)) for d in dils),
        scratch_shapes=[pltpu.VMEM((tm, LANES), F32)],
        compiler_params=_params(("parallel",)),
        name="odd_proj",
    )(x, tab, w)


def _dil_p_kernel(q_ref, kp_ref, kc_ref, vp_ref, vc_ref, o_ref, lse_ref):
    tq = SWA_KEYS
    mt = pl.program_id(2)
    i = lax.broadcasted_iota(jnp.int32, (tq, 2 * tq), 0)
    kk = lax.broadcasted_iota(jnp.int32, (tq, 2 * tq), 1)
    first_ok = jnp.where(mt > 0, 0, tq)
    band = (kk >= i) & (kk <= i + tq) & (kk >= first_ok)
    lane = lax.broadcasted_iota(jnp.int32, (tq, LANES), 1)
    stats = jnp.zeros((tq, LANES), F32)
    for h in range(C_HEADS):
        hs = slice(h * 128, (h + 1) * 128)
        q = (q_ref[0, 0, :, hs] * C_SCALE).astype(BF16)
        k = jnp.concatenate([kp_ref[0, 0, :, hs], kc_ref[0, 0, :, hs]], axis=0).astype(BF16)
        v = jnp.concatenate([vp_ref[0, 0, :, hs], vc_ref[0, 0, :, hs]], axis=0).astype(BF16)
        s = jnp.where(band, _dot_nt(q, k), NEG_INF)
        m = jnp.max(s, axis=1, keepdims=True)
        p = jnp.exp(s - m)
        l = jnp.sum(p, axis=1, keepdims=True)
        o_ref[0, 0, :, hs] = _dot(p.astype(BF16), v) / l
        stats = jnp.where(lane == h, m + jnp.log(l), stats)
    lse_ref[0, 0] = stats


def _dilated_prompt(hv):
    bsz, dil, sd, _ = hv.shape
    tq = SWA_KEYS
    nt = sd // tq
    prev = lambda mt: jnp.maximum(mt - 1, 0)
    return pl.pallas_call(
        _dil_p_kernel,
        out_shape=(jax.ShapeDtypeStruct((bsz, dil, sd, 512), F32),
                   jax.ShapeDtypeStruct((bsz, dil, sd, 128), F32)),
        grid=(bsz, dil, nt),
        in_specs=[pl.BlockSpec((1, 1, tq, 512), lambda b, r, mt: (b, r, mt, 0)),
                  pl.BlockSpec((1, 1, tq, 512), lambda b, r, mt: (b, r, prev(mt), 1)),
                  pl.BlockSpec((1, 1, tq, 512), lambda b, r, mt: (b, r, mt, 1)),
                  pl.BlockSpec((1, 1, tq, 512), lambda b, r, mt: (b, r, prev(mt), 2)),
                  pl.BlockSpec((1, 1, tq, 512), lambda b, r, mt: (b, r, mt, 2))],
        out_specs=(pl.BlockSpec((1, 1, tq, 512), lambda b, r, mt: (b, r, mt, 0)),
                   pl.BlockSpec((1, 1, tq, 128), lambda b, r, mt: (b, r, mt, 0))),
        compiler_params=_params(("parallel", "parallel", "arbitrary")),
        name="dilated_prompt",
    )(hv, hv, hv, hv, hv)


def _dil_s_kernel(q0_ref, q1_ref, q2_ref, n0_ref, n1_ref, n2_ref, c0_ref, c1_ref, c2_ref,
                  o0_ref, o1_ref, o2_ref, l0_ref, l1_ref, l2_ref):
    q_refs = (q0_ref, q1_ref, q2_ref)
    n_refs = (n0_ref, n1_ref, n2_ref)
    c_refs = (c0_ref, c1_ref, c2_ref)
    o_refs = (o0_ref, o1_ref, o2_ref)
    l_refs = (l0_ref, l1_ref, l2_ref)
    for g in range(N_SWA):
        q = (q_refs[g][0] * C_SCALE).astype(BF16)
        cache = c_refs[g][0]
        n_old = cache.shape[0] * cache.shape[1]
        rows = jnp.concatenate([cache.reshape(n_old, C_HEAD_DIM), n_refs[g][0]], axis=0).astype(BF16)
        s = _dot_nt(q, rows)
        col = lax.broadcasted_iota(jnp.int32, s.shape, 1)
        row = lax.broadcasted_iota(jnp.int32, s.shape, 0)
        tok, head = row >> 2, row & 3
        new_col = col - n_old
        is_old = col < n_old
        if g == 0:
            ok_old = ((col & 7) == head) & ((col >> 3) >= tok)
            ok_new = ((new_col & 7) == head) & ((new_col >> 3) <= tok)
        else:
            ok_old = (col & 31) == tok * 8 + head
            ok_new = new_col == tok * 8 + head
        s = jnp.where((is_old & ok_old) | ((col >= n_old) & ok_new), s, NEG_INF)
        m = jnp.max(s, axis=1, keepdims=True)
        p = jnp.exp(s - m)
        l = jnp.sum(p, axis=1, keepdims=True)
        pv = pltpu.roll(p, 4, 1).astype(BF16)
        o_refs[g][0] = _dot(pv, rows) / l
        l_refs[g][0] = jnp.broadcast_to(m + jnp.log(l), (q.shape[0], LANES))


def _dilated_sample(q_list, new_list, caches, row_base):
    dbs = q_list[0].shape[0]
    sds = jax.ShapeDtypeStruct((dbs, 16, LANES), F32)
    idx = lambda b: (b, 0, 0)
    cidx = lambda b: (row_base + b, 0, 0, 0)
    cache_rows = (8, 32, 32)
    outs = pl.pallas_call(
        _dil_s_kernel,
        out_shape=(sds,) * 6,
        grid=(dbs,),
        in_specs=[pl.BlockSpec((1, 16, LANES), idx)] * 3 + [pl.BlockSpec((1, LANES, LANES), idx)] * 3
        + [pl.BlockSpec((1, SWA_KEYS, r, C_HEAD_DIM), cidx) for r in cache_rows],
        out_specs=(pl.BlockSpec((1, 16, LANES), idx),) * 6,
        compiler_params=_params(("parallel",)),
        name="dilated_sample",
    )(*q_list, *new_list, *caches)
    return outs[:3], outs[3:]


def _odd_out_kernel(o0_ref, o1_ref, o2_ref, l0_ref, l1_ref, l2_ref, w_ref, r_ref, g_ref, b_ref, out_ref, scr):
    tm = r_ref.shape[0]

    def token_major(ref, cols):
        d = ref.shape[1]
        if d == 1:
            return ref[0, 0, :, cols]
        for r in range(d):
            scr[pl.ds(r, tm // d, stride=d), :] = ref[0, r, :, cols]
        return scr[...]

    lses = [token_major(ref, slice(0, LANES)) for ref in (l0_ref, l1_ref, l2_ref)]
    parts = []
    for h in range(C_HEADS):
        hs = slice(h * 128, (h + 1) * 128)
        a0, a1, a2 = (ls[:, h:h + 1] for ls in lses)
        mx = jnp.maximum(jnp.maximum(a0, a1), a2)
        e0, e1, e2 = jnp.exp(a0 - mx), jnp.exp(a1 - mx), jnp.exp(a2 - mx)
        den = e0 + e1 + e2
        parts.append((e0 / den) * token_major(o0_ref, hs) + (e1 / den) * token_major(o1_ref, hs)
                     + (e2 / den) * token_major(o2_ref, hs))
    comb = jnp.concatenate(parts, axis=1).astype(BF16)
    out_ref[...] = _layer_norm(ALPHA * r_ref[...] + _dot(comb, w_ref[...]), g_ref[...], b_ref[...])


def _odd_out(outs, lses, w, resid, g, b, tm):
    m = resid.shape[0]
    bsz = outs[0].shape[0]
    tpb = m // bsz // tm
    row = lambda i: (i, 0)
    const = lambda i: (0, 0)
    res = lambda a: pl.BlockSpec((1, a.shape[1], tm // a.shape[1], a.shape[3]), lambda i: (i // tpb, 0, i % tpb, 0))
    return pl.pallas_call(
        _odd_out_kernel,
        out_shape=jax.ShapeDtypeStruct((m, D_MODEL), F32),
        grid=(m // tm,),
        in_specs=[res(a) for a in outs] + [res(a) for a in lses]
        + [pl.BlockSpec(w.shape, const), pl.BlockSpec((tm, D_MODEL), row),
           pl.BlockSpec((1, D_MODEL), const), pl.BlockSpec((1, D_MODEL), const)],
        out_specs=pl.BlockSpec((tm, D_MODEL), row),
        scratch_shapes=[pltpu.VMEM((tm, LANES), F32)],
        compiler_params=_params(("parallel",)),
        name="odd_out",
    )(*outs, *lses, w, resid, g, b)


def _mem_heads(q, kv_ref_val):
    outs = []
    for h in range(MEM_HEADS):
        k = kv_ref_val[:, h * 128:(h + 1) * 128]
        v = kv_ref_val[:, 512 + h * 128:512 + (h + 1) * 128]
        s = _dot_nt(q[:, h * 128:(h + 1) * 128].astype(BF16), k)
        m = jnp.max(s, axis=1, keepdims=True)
        p = jnp.exp(s - m)
        l = jnp.sum(p, axis=1, keepdims=True)
        outs.append(_dot(p.astype(BF16), v) / l)
    return jnp.concatenate(outs, axis=1)


def _mem_p_kernel(x_ref, kv_ref, wq_ref, wo_ref, g_ref, b_ref, o_ref):
    x = x_ref[...]
    q = _dot(x.astype(BF16), wq_ref[...]) * MEM_SCALE
    att = _mem_heads(q, kv_ref[0])
    o_ref[...] = _layer_norm(ALPHA * x + _dot(att.astype(BF16), wo_ref[...]), g_ref[...], b_ref[...])


def _mem_prompt(x, kv, wq, wo, g, b, tiles_per_batch, tm):
    m = x.shape[0]
    row = lambda i: (i, 0)
    const = lambda i: (0, 0)
    return pl.pallas_call(
        _mem_p_kernel,
        out_shape=jax.ShapeDtypeStruct((m, D_MODEL), F32),
        grid=(m // tm,),
        in_specs=[pl.BlockSpec((tm, D_MODEL), row),
                  pl.BlockSpec((1, N_MEM, 1024), lambda i: (i // tiles_per_batch, 0, 0)),
                  pl.BlockSpec(wq.shape, const), pl.BlockSpec(wo.shape, const),
                  pl.BlockSpec((1, D_MODEL), const), pl.BlockSpec((1, D_MODEL), const)],
        out_specs=pl.BlockSpec((tm, D_MODEL), row),
        compiler_params=_params(("parallel",)),
        name="mem_prompt",
    )(x, kv, wq, wo, g, b)


def _mem_s_kernel(q_ref, kv_ref, o_ref):
    rows = kv_ref[0].astype(BF16)
    q = (q_ref[0] * MEM_SCALE).astype(BF16)
    s = _dot_nt(q, rows)
    col = lax.broadcasted_iota(jnp.int32, s.shape, 1)
    head = lax.broadcasted_iota(jnp.int32, s.shape, 0) & 3
    s = jnp.where((col & 7) == head, s, NEG_INF)
    m = jnp.max(s, axis=1, keepdims=True)
    p = jnp.exp(s - m)
    l = jnp.sum(p, axis=1, keepdims=True)
    pv = pltpu.roll(p, 4, 1).astype(BF16)
    o_ref[0] = _dot(pv, rows) / l


def _mem_sample_attn(q, kv, row_base):
    dbs = q.shape[0]
    idx = lambda b: (b, 0, 0)
    return pl.pallas_call(
        _mem_s_kernel,
        out_shape=jax.ShapeDtypeStruct(q.shape, F32),
        grid=(dbs,),
        in_specs=[pl.BlockSpec((1,) + q.shape[1:], idx),
                  pl.BlockSpec((1,) + kv.shape[1:], lambda b: (row_base + b, 0, 0))],
        out_specs=pl.BlockSpec((1,) + q.shape[1:], idx),
        compiler_params=_params(("parallel",)),
        name="mem_sample",
    )(q, kv)


def _router_kernel(x_ref, wr_ref, rb_ref, gate_ref):
    tm = x_ref.shape[0]
    per_grp = N_EXPERTS // N_EXPERT_GROUPS
    logits = _dot_nt(wr_ref[...], x_ref[...], precision=HIGHEST)
    s = _sigmoid(logits)
    sb = s + rb_ref[...]
    sub = lax.broadcasted_iota(jnp.int32, (per_grp, tm), 0).astype(F32)
    slabs, gscore = [], []
    for gi in range(N_EXPERT_GROUPS):
        slab = sb[gi * per_grp:(gi + 1) * per_grp, :]
        m1 = jnp.max(slab, axis=0, keepdims=True)
        f1 = jnp.min(jnp.where(slab == m1, sub, float(per_grp)), axis=0, keepdims=True)
        m2 = jnp.max(jnp.where(sub == f1, -jnp.inf, slab), axis=0, keepdims=True)
        slabs.append(slab)
        gscore.append(m1 + m2)
    masked = []
    for gi in range(N_EXPERT_GROUPS):
        beaten = jnp.zeros((1, tm), F32)
        for gj in range(N_EXPERT_GROUPS):
            if gj < gi:
                beaten = beaten + jnp.where(gscore[gj] >= gscore[gi], 1.0, 0.0)
            elif gj > gi:
                beaten = beaten + jnp.where(gscore[gj] > gscore[gi], 1.0, 0.0)
        masked.append(jnp.where(beaten < TOPK_GROUPS - 0.5, slabs[gi], NEG_INF))
    esel = _top_k_mask(jnp.concatenate(masked, axis=0), MOE_TOPK, 0)
    w = s * esel
    w = w / jnp.sum(w, axis=0, keepdims=True) * ROUTED_SCALE
    wpad = jnp.concatenate([w, jnp.zeros((LANES - N_EXPERTS, tm), F32)], axis=0)
    gate_ref[...] = wpad.T


def _router(x, wr_t, rb, tm):
    m = x.shape[0]
    tm = min(tm, m)
    return pl.pallas_call(
        _router_kernel,
        out_shape=jax.ShapeDtypeStruct((m, LANES), F32),
        grid=(m // tm,),
        in_specs=[pl.BlockSpec((tm, D_MODEL), lambda i: (i, 0)),
                  pl.BlockSpec(wr_t.shape, lambda i: (0, 0)),
                  pl.BlockSpec(rb.shape, lambda i: (0, 0))],
        out_specs=pl.BlockSpec((tm, LANES), lambda i: (i, 0)),
        compiler_params=_params(("parallel",)),
        name="router",
    )(x, wr_t, rb)


def _moe_kernel(x_ref, gate_ref, wg_ref, wu_ref, wd_ref, sg_ref, su_ref, sd_ref, g_ref, b_ref,
                o_ref, xb_s):
    e = pl.program_id(1)

    @pl.when(e == 0)
    def _():
        xb0 = x_ref[...].astype(BF16)
        xb_s[...] = xb0
        hg = _dot(xb0, sg_ref[...])
        hs = hg * _sigmoid(hg) * _dot(xb0, su_ref[...])
        o_ref[...] = _dot(hs.astype(BF16), sd_ref[...])

    xb = xb_s[...]
    hg = _dot(xb, wg_ref[0])
    h = hg * _sigmoid(hg) * _dot(xb, wu_ref[0])
    gate = gate_ref[...]
    lane = lax.broadcasted_iota(jnp.int32, gate.shape, 1)
    gcol = jnp.sum(jnp.where(lane == e, gate, 0.0), axis=1, keepdims=True)
    o_ref[...] += _dot((h * gcol).astype(BF16), wd_ref[0])

    @pl.when(e == pl.num_programs(1) - 1)
    def _():
        o_ref[...] = _layer_norm(ALPHA * x_ref[...] + o_ref[...], g_ref[...], b_ref[...])


def _moe(x, gate, wg, wu, wd, sg, su, sd, g, b, tm):
    m = x.shape[0]
    tm = min(tm, m)
    row = lambda i, e: (i, 0)
    const = lambda i, e: (0, 0)
    return pl.pallas_call(
        _moe_kernel,
        out_shape=jax.ShapeDtypeStruct((m, D_MODEL), F32),
        grid=(m // tm, N_EXPERTS),
        in_specs=[pl.BlockSpec((tm, D_MODEL), row),
                  pl.BlockSpec((tm, LANES), row),
                  pl.BlockSpec((1, D_MODEL, D_EXPERT), lambda i, e: (e, 0, 0)),
                  pl.BlockSpec((1, D_MODEL, D_EXPERT), lambda i, e: (e, 0, 0)),
                  pl.BlockSpec((1, D_EXPERT, D_MODEL), lambda i, e: (e, 0, 0)),
                  pl.BlockSpec(sg.shape, const), pl.BlockSpec(su.shape, const),
                  pl.BlockSpec(sd.shape, const),
                  pl.BlockSpec((1, D_MODEL), const), pl.BlockSpec((1, D_MODEL), const)],
        out_specs=pl.BlockSpec((tm, D_MODEL), row),
        scratch_shapes=[pltpu.VMEM((tm, D_MODEL), BF16)],
        compiler_params=_params(("parallel", "arbitrary")),
        name="moe",
    )(x, gate, wg, wu, wd, sg, su, sd, g, b)


def _rope_cs(pos, half):
    inv = ROPE_THETA ** (-jnp.arange(half, dtype=F32) / half)
    ang = pos.astype(F32)[:, None] * inv[None, :]
    return jnp.cos(ang), jnp.sin(ang)


def _even_table(pos):
    n = pos.shape[0]
    c32, s32 = _rope_cs(pos, 32)
    c16, s16 = _rope_cs(pos, 16)
    one = lambda w: jnp.ones((n, w), F32)
    zero = lambda w: jnp.zeros((n, w), F32)
    cos_a = jnp.tile(c32, (1, 4))
    sin_a = jnp.tile(jnp.concatenate([-s32, s32], axis=1), (1, 2))
    cos_b = jnp.concatenate([one(64), c16, c16, one(32)], axis=1)
    sin_b = jnp.concatenate([zero(64), -s16, s16, zero(32)], axis=1)
    cos_c = jnp.concatenate([c16, c16, one(96)], axis=1)
    sin_c = jnp.concatenate([-s16, s16, zero(96)], axis=1)
    return jnp.concatenate([cos_a, sin_a, cos_b, sin_b, cos_c, sin_c], axis=1)


def _odd_table(pos):
    c64, s64 = _rope_cs(pos, 64)
    return jnp.concatenate([c64, c64, -s64, s64], axis=1)


_MOBA_HEAD_PERM = (0, 2, 1, 3, 4, 6, 5, 7)


def _even_weights(w_in, w_uq, w_uk, w_uv, w_out):
    hp = np.asarray(_MOBA_HEAD_PERM)
    qcols = (hp[:, None] * A_HEAD_DIM + np.arange(A_HEAD_DIM)[None, :]).reshape(-1)
    w_q = w_in[:, :512][:, qcols]
    w_in_p = jnp.concatenate(
        [w_q, w_in[:, 512:], jnp.zeros((D_MODEL, EVEN_W - w_in.shape[1]), F32)], axis=1).astype(BF16)
    uq = w_uq.reshape(MLA_Q_RANK, B_HEADS, MLA_D_NOPE + MLA_D_ROPE)
    w_uq_p = jnp.pad(uq, ((0, 0), (0, 0), (0, 32))).reshape(MLA_Q_RANK, B_HEADS * 128).astype(BF16)
    eye = jnp.eye(MLA_D_ROPE, dtype=F32)
    k_top = jnp.pad(w_uk, ((0, 0), (0, 0), (0, 64)))
    k_mid = jnp.broadcast_to(jnp.pad(eye, ((0, 0), (64, 32)))[:, None, :], (MLA_D_ROPE, B_HEADS, 128))
    k_all = jnp.concatenate([k_top, k_mid, jnp.zeros((96, B_HEADS, 128), F32)], axis=0).reshape(384, 1024)
    v_all = jnp.concatenate([w_uv.reshape(MLA_KV_RANK, 512), jnp.zeros((128, 512), F32)], axis=0)
    w_kv = jnp.concatenate([k_all, v_all], axis=1).astype(BF16)
    blocks = []
    for h in range(B_HEADS):
        blk = jnp.zeros((128, 384), F32)
        blk = blk.at[0:64, 0:256].set(w_uk[:, h, :].T)
        blk = blk.at[64:96, 256:288].set(eye)
        blocks.append(blk)
    w_abs = jax.scipy.linalg.block_diag(*blocks).astype(BF16)
    w_uv_bd = jax.scipy.linalg.block_diag(*[w_uv[:, h, :] for h in range(B_HEADS)]).astype(BF16)
    rows = (hp[:, None] * A_HEAD_DIM + np.arange(A_HEAD_DIM)[None, :]).reshape(-1)
    w_out_moba_p = w_out[:512][rows].astype(BF16)
    w_out_moba = w_out[:512].astype(BF16)
    w_out_mla = w_out[512:].astype(BF16)
    return w_in_p, w_uq_p, w_kv, w_abs, w_uv_bd, w_out_moba_p, w_out_moba, w_out_mla


def kernel(x_prompt, x_sample, mem_prompt, cache_moba_k, cache_moba_v, cache_mla, cache_swa0_kv, cache_swa1_kv, cache_swa2_kv, cache_mem_kv, page_table, ln_gain, ln_bias, w_in_even, w_out_even, mla_q_norm, mla_w_uq, mla_kv_norm, mla_w_uk, mla_w_uv, w_in_odd, w_out_odd, w_mem_q, w_mem_kv, w_mem_o, w_router, router_bias, w_exp_gate, w_exp_up, w_exp_down, w_sh_gate, w_sh_up, w_sh_down):
    bp, sp, _ = x_prompt.shape
    bs, ss, _ = x_sample.shape
    n_p = bp * sp
    n_s = bs * ss
    tm = 256
    n_pool = cache_moba_k.shape[1]
    swa_caches = (cache_swa0_kv, cache_swa1_kv, cache_swa2_kv)

    xp = x_prompt.reshape(n_p, D_MODEL)
    xs = x_sample.reshape(n_s, D_MODEL)
    pos_p = jnp.arange(sp)
    pos_s = jnp.tile(PAST_LEN + jnp.arange(ss), n_s // ss)
    tab_even_p, tab_even_s = _even_table(pos_p), _even_table(pos_s)
    tab_odd_p, tab_odd_s = _odd_table(pos_p), _odd_table(pos_s)
    mem2 = mem_prompt.reshape(bp * N_MEM, D_MODEL)

    ck_pages = jnp.transpose(cache_moba_k, (0, 1, 3, 4, 2)).reshape(-1, A_KV_HEADS * A_HEAD_DIM, PAGE_SIZE)
    cv_pages = jnp.transpose(cache_moba_v, (0, 1, 3, 4, 2)).reshape(-1, A_KV_HEADS * A_HEAD_DIM, PAGE_SIZE)
    cm_pages = jnp.transpose(cache_mla, (0, 1, 3, 2)).reshape(-1, MLA_ROW, PAGE_SIZE)
    swa_rows = [c.reshape(-1, SWA_KEYS, dil * 2 * C_HEADS, C_HEAD_DIM)
                for c, (win, dil) in zip(swa_caches, SWA_CONFIGS)]
    mem_rows = cache_mem_kv.reshape(-1, N_MEM * 2 * MEM_HEADS, MEM_HEAD_DIM)

    moba_k_p, moba_v_p, mla_p, mem_p = [], [], [], []
    moba_k_s, moba_v_s, mla_s = [], [], []
    swa_p = [[] for _ in range(N_SWA)]
    swa_s = [[] for _ in range(N_SWA)]

    for l in range(DEPTH):
        ln_g = [ln_gain[l, i].reshape(1, D_MODEL) for i in range(3)]
        ln_b = [ln_bias[l, i].reshape(1, D_MODEL) for i in range(3)]
        if l % 2 == 0:
            e = l // 2
            (w_in_p, w_uq_p, w_kv, w_abs, w_uv_bd, w_out_moba_p, w_out_moba, w_out_mla) = _even_weights(
                w_in_even[e], mla_w_uq[e], mla_w_uk[e], mla_w_uv[e], w_out_even[e])
            qn = mla_q_norm[e].reshape(1, MLA_Q_RANK)
            kvn = mla_kv_norm[e].reshape(1, MLA_KV_RANK)
            qa, ka, va, km, mla, qm, kx, vat, vmt = _even_proj(
                xp, tab_even_p, sp // tm, w_in_p, qn, w_uq_p, kvn, w_kv, tm)
            o_moba = _moba_prompt(qa, ka, vat, km.reshape(n_p // MOBA_BLOCK, 256), bp, sp)
            o_mla = _mla_prompt(qm, kx, vmt, bp, sp)
            xp = _mm_res_ln([o_moba, o_mla], [w_out_moba_p, w_out_mla], xp, ln_g[0], ln_b[0], tm)
            moba_k_p.append(ka.reshape(bp, sp, A_KV_HEADS, A_HEAD_DIM))
            moba_v_p.append(va.reshape(bp, sp, A_KV_HEADS, A_HEAD_DIM))
            mla_p.append(mla.reshape(bp, sp, MLA_ROW))
            qa, ka, va, km, mla, qm, kx, vat, vmt = _even_proj(
                xs, tab_even_s, n_s // tm, w_in_p, qn, w_uq_p, kvn, w_kv, tm)
            inv = np.argsort(np.asarray(_MOBA_HEAD_PERM))
            q5 = qa.reshape(bs, ss, A_HEADS, A_HEAD_DIM)[:, :, inv].reshape(bs, ss, A_KV_HEADS, 2, A_HEAD_DIM)
            eye_kv = jnp.eye(A_KV_HEADS, dtype=F32)[None, :, None, None, :, None]
            q_rows = (q5.transpose(0, 2, 1, 3, 4)[:, :, :, :, None, :] * eye_kv).reshape(
                bs, A_KV_HEADS * ss * 2, A_KV_HEADS * A_HEAD_DIM)
            o_ms = _moba_sample(page_table, q_rows, ka.reshape(bs, ss, 256), va.reshape(bs, ss, 256),
                                ck_pages, cv_pages, e * n_pool)
            o_ms = o_ms.reshape(bs, ss, 2, A_KV_HEADS, A_HEAD_DIM).transpose(0, 1, 3, 2, 4).reshape(n_s, 512)
            q_abs = _matmul(qm, w_abs, tm, BF16).reshape(bs, ss * B_HEADS, 384)
            o_lat = _mla_sample(page_table, q_abs, mla.reshape(bs, ss, MLA_ROW), cm_pages, e * n_pool)
            o_mls = _matmul(o_lat.reshape(n_s, B_HEADS * MLA_KV_RANK), w_uv_bd, tm)
            xs = _mm_res_ln([o_ms, o_mls], [w_out_moba, w_out_mla], xs, ln_g[0], ln_b[0], tm)
            moba_k_s.append(ka.reshape(bs, ss, A_KV_HEADS, A_HEAD_DIM))
            moba_v_s.append(va.reshape(bs, ss, A_KV_HEADS, A_HEAD_DIM))
            mla_s.append(mla.reshape(bs, ss, MLA_ROW))
        else:
            o = l // 2
            w_odd = w_in_odd[o].astype(BF16)
            w_oo = w_out_odd[o].astype(BF16)
            dils = tuple(dil for _, dil in SWA_CONFIGS)
            hvs = _odd_proj(xp, tab_odd_p, sp // tm, w_odd, tm, bp, dils)
            outs, lses = [], []
            for g, (win, dil) in enumerate(SWA_CONFIGS):
                og, lg = _dilated_prompt(hvs[g])
                outs.append(og)
                lses.append(lg)
                keep = min(win, sp)
                tail = hvs[g][:, :, (sp - keep) // dil:, 512:1536]
                swa_p[g].append(tail.transpose(0, 2, 1, 3).reshape(bp, keep, 2, C_HEADS, C_HEAD_DIM))
            xp = _odd_out(outs, lses, w_oo, xp, ln_g[0], ln_b[0], tm)
            hs_g = _odd_proj(xs, tab_odd_s, n_s // tm, w_odd, tm, 1, (1, 1, 1))
            q_list = [a[0, 0, :, 0:512].reshape(bs, ss * C_HEADS, C_HEAD_DIM) for a in hs_g]
            kv_new = [a[0, 0, :, 512:1536].reshape(bs, ss * 2 * C_HEADS, C_HEAD_DIM) for a in hs_g]
            new_list = [jnp.pad(a, ((0, 0), (0, LANES - a.shape[1]), (0, 0))) for a in kv_new]
            outs_s, lses_s = _dilated_sample(q_list, new_list, swa_rows, o * bs)
            lses_s = [jnp.pad(a[:, :, 0].reshape(n_s, C_HEADS), ((0, 0), (0, LANES - C_HEADS))) for a in lses_s]
            xs = _odd_out([a.reshape(1, 1, n_s, 512) for a in outs_s], [a.reshape(1, 1, n_s, LANES) for a in lses_s],
                          w_oo, xs, ln_g[0], ln_b[0], tm)
            for g in range(N_SWA):
                swa_s[g].append(kv_new[g].reshape(bs, ss, 2, C_HEADS, C_HEAD_DIM))
        w_mq = w_mem_q[l].astype(BF16)
        w_mo = w_mem_o[l].astype(BF16)
        mkv = _matmul(mem2, w_mem_kv[l].astype(BF16), tm)
        mem_p.append(mkv.reshape(bp, N_MEM, 2, MEM_HEADS, MEM_HEAD_DIM))
        xp = _mem_prompt(xp, mkv.reshape(bp, N_MEM, 1024).astype(BF16), w_mq, w_mo,
                         ln_g[1], ln_b[1], sp // tm, tm)
        q_s = _matmul(xs, w_mq, tm).reshape(bs, ss * MEM_HEADS, MEM_HEAD_DIM)
        att_s = _mem_sample_attn(q_s, mem_rows, l * bs)
        xs = _mm_res_ln([att_s.reshape(n_s, 512)], [w_mo], xs, ln_g[1], ln_b[1], tm)
        wr_t = w_router[l].T
        rb = router_bias[l].reshape(N_EXPERTS, 1)
        wg, wu, wd = w_exp_gate[l].astype(BF16), w_exp_up[l].astype(BF16), w_exp_down[l].astype(BF16)
        sg, su, sd = w_sh_gate[l].astype(BF16), w_sh_up[l].astype(BF16), w_sh_down[l].astype(BF16)
        gate_p = _router(xp, wr_t, rb, 512)
        xp = _moe(xp, gate_p, wg, wu, wd, sg, su, sd, ln_g[2], ln_b[2], 1024)
        gate_s = _router(xs, wr_t, rb, 512)
        xs = _moe(xs, gate_s, wg, wu, wd, sg, su, sd, ln_g[2], ln_b[2], 512)

    return (xp.reshape(bp, sp, D_MODEL), xs.reshape(bs, ss, D_MODEL),
            jnp.stack(moba_k_p), jnp.stack(moba_v_p), jnp.stack(mla_p),
            jnp.stack(swa_p[0]), jnp.stack(swa_p[1]), jnp.stack(swa_p[2]), jnp.stack(mem_p),
            jnp.stack(moba_k_s), jnp.stack(moba_v_s), jnp.stack(mla_s),
            jnp.stack(swa_s[0]), jnp.stack(swa_s[1]), jnp.stack(swa_s[2]))
```

```python
import functools

import numpy as np
import jax
import jax.numpy as jnp
from jax import lax
from jax.experimental import pallas as pl
from jax.experimental.pallas import tpu as pltpu

F32 = jnp.float32
BF16 = jnp.bfloat16
HIGHEST = lax.Precision.HIGHEST

D_MODEL = 1024
DEPTH = 4
PAST_LEN = 8192
PAGE_SIZE = 128
A_HEADS = 8
A_KV_HEADS = 4
A_HEAD_DIM = 64
MOBA_BLOCK = 256
MOBA_TOPK = 3
B_HEADS = 8
MLA_Q_RANK = 384
MLA_KV_RANK = 256
MLA_D_NOPE = 64
MLA_D_ROPE = 32
MLA_D_V = 64
MLA_ROW = MLA_KV_RANK + MLA_D_ROPE
SWA_CONFIGS = ((128, 1), (512, 4), (2048, 16))
N_SWA = 3
C_HEADS = 4
C_HEAD_DIM = 128
SWA_KEYS = 128
MLA_KV_TILE = 256
MEM_SAMPLE_PER_STEP = 4
N_MEM = 256
MEM_HEADS = 4
MEM_HEAD_DIM = 128
N_EXPERTS = 64
N_EXPERT_GROUPS = 8
TOPK_GROUPS = 4
MOE_TOPK = 8
D_EXPERT = 256
ROUTED_SCALE = 2.5
ROPE_THETA = 10000.0
LN_EPS = 1e-5
RMS_EPS = 1e-6
NEG_INF = -1e30
ALPHA = (2 * DEPTH) ** 0.25
A_SCALE = A_HEAD_DIM ** -0.5
MLA_SCALE = (MLA_D_NOPE + MLA_D_ROPE) ** -0.5
C_SCALE = C_HEAD_DIM ** -0.5
MEM_SCALE = MEM_HEAD_DIM ** -0.5

LANES = 128
EVEN_W = 1792
ODD_W = N_SWA * 3 * C_HEADS * C_HEAD_DIM
VMEM_LIMIT = 56 * 1024 * 1024

_NT = (((1,), (1,)), ((), ()))


def _params(sem, vmem=VMEM_LIMIT):
    return pltpu.CompilerParams(dimension_semantics=sem, vmem_limit_bytes=vmem)


def _dot(a, b):
    return jnp.dot(a, b, preferred_element_type=F32)


def _dot_nt(a, b, precision=None):
    return lax.dot_general(a, b, _NT, precision=precision, preferred_element_type=F32)


def _layer_norm(z, g, b):
    mu = jnp.mean(z, axis=-1, keepdims=True)
    zc = z - mu
    var = jnp.mean(zc * zc, axis=-1, keepdims=True)
    return zc * lax.rsqrt(var + LN_EPS) * g + b


def _rms(x, g):
    return x * lax.rsqrt(jnp.mean(x * x, axis=-1, keepdims=True) + RMS_EPS) * g


def _sigmoid(x):
    return 1.0 / (1.0 + jnp.exp(-x))


def _top_k_mask(vals, k, axis):
    n = vals.shape[axis]
    idx = lax.broadcasted_iota(jnp.int32, vals.shape, axis).astype(F32)
    sel = jnp.zeros(vals.shape, F32)
    for _ in range(k):
        mx = jnp.max(vals, axis=axis, keepdims=True)
        first = jnp.min(jnp.where(vals == mx, idx, float(n)), axis=axis, keepdims=True)
        hit = idx == first
        sel = jnp.where(hit, 1.0, sel)
        vals = jnp.where(hit, -jnp.inf, vals)
    return sel


def _rope_block(yb, cos, sin, half, lane):
    fwd = pltpu.roll(yb, LANES - half, 1)
    bwd = pltpu.roll(yb, half, 1)
    first = (lane & (2 * half - 1)) < half
    return yb * cos + jnp.where(first, fwd, bwd) * sin


def _mm_kernel(x_ref, w_ref, o_ref):
    o_ref[...] = _dot(x_ref[...].astype(BF16), w_ref[...]).astype(o_ref.dtype)


def _matmul(x, w, tm, out_dtype=F32):
    m, k = x.shape
    n = w.shape[1]
    return pl.pallas_call(
        _mm_kernel,
        out_shape=jax.ShapeDtypeStruct((m, n), out_dtype),
        grid=(m // tm,),
        in_specs=[pl.BlockSpec((tm, k), lambda i: (i, 0)),
                  pl.BlockSpec((k, n), lambda i: (0, 0))],
        out_specs=pl.BlockSpec((tm, n), lambda i: (i, 0)),
        compiler_params=_params(("parallel",)),
        name="matmul",
    )(x, w)


def _mm_res_ln_kernel(n_in, *refs):
    a_refs = refs[:n_in]
    w_refs = refs[n_in:2 * n_in]
    r_ref, g_ref, b_ref, o_ref = refs[2 * n_in:]
    acc = None
    for a_ref, w_ref in zip(a_refs, w_refs):
        t = _dot(a_ref[...].astype(BF16), w_ref[...])
        acc = t if acc is None else acc + t
    o_ref[...] = _layer_norm(ALPHA * r_ref[...] + acc, g_ref[...], b_ref[...])


def _mm_res_ln(a_list, w_list, resid, g, b, tm):
    m = resid.shape[0]
    n_in = len(a_list)
    in_specs = [pl.BlockSpec((tm, a.shape[1]), lambda i: (i, 0)) for a in a_list]
    in_specs += [pl.BlockSpec(w.shape, lambda i: (0, 0)) for w in w_list]
    in_specs += [pl.BlockSpec((tm, D_MODEL), lambda i: (i, 0)),
                 pl.BlockSpec((1, D_MODEL), lambda i: (0, 0)),
                 pl.BlockSpec((1, D_MODEL), lambda i: (0, 0))]
    return pl.pallas_call(
        functools.partial(_mm_res_ln_kernel, n_in),
        out_shape=jax.ShapeDtypeStruct((m, D_MODEL), F32),
        grid=(m // tm,),
        in_specs=in_specs,
        out_specs=pl.BlockSpec((tm, D_MODEL), lambda i: (i, 0)),
        compiler_params=_params(("parallel",)),
        name="mm_res_ln",
    )(*a_list, *w_list, resid, g, b)


def _even_proj_kernel(x_ref, tab_ref, win_ref, qn_ref, wuq_ref, kvn_ref, wkv_ref,
                      qa_ref, ka_ref, va_ref, km_ref, mla_ref, qm_ref, kx_ref, vat_ref, vmt_ref):
    tm = x_ref.shape[0]
    lane = lax.broadcasted_iota(jnp.int32, (tm, LANES), 1)
    y = _dot(x_ref[...].astype(BF16), win_ref[...])
    cos_a, sin_a = tab_ref[:, 0:128], tab_ref[:, 128:256]
    cos_b, sin_b = tab_ref[:, 256:384], tab_ref[:, 384:512]
    cos_c, sin_c = tab_ref[:, 512:640], tab_ref[:, 640:768]
    for j in range(4):
        qa_ref[:, j * 128:(j + 1) * 128] = _rope_block(
            y[:, j * 128:(j + 1) * 128], cos_a, sin_a, 32, lane)
    for j in range(2):
        kb = _rope_block(y[:, 512 + j * 128:640 + j * 128], cos_a, sin_a, 32, lane)
        ka_ref[:, j * 128:(j + 1) * 128] = kb
        km_ref[0, :, j * 128:(j + 1) * 128] = jnp.mean(kb, axis=0, keepdims=True)
    va_ref[...] = y[:, 768:1024]
    vat_ref[...] = y[:, 768:1024].T.astype(BF16)
    cq = _rms(y[:, 1024:1408], qn_ref[...])
    qm = _dot(cq.astype(BF16), wuq_ref[...])
    for j in range(B_HEADS):
        qj = _rope_block(qm[:, j * 128:(j + 1) * 128], cos_b, sin_b, 16, lane)
        qm_ref[:, j * 128:(j + 1) * 128] = (qj * MLA_SCALE).astype(BF16)
    ckv = _rms(y[:, 1408:1664], kvn_ref[...])
    kpe = _rope_block(y[:, 1664:1792], cos_c, sin_c, 16, lane)
    mla_ref[:, 0:MLA_KV_RANK] = ckv
    mla_ref[:, MLA_KV_RANK:MLA_ROW] = kpe[:, 0:MLA_D_ROPE]
    ck = jnp.concatenate([ckv, kpe], axis=1).astype(BF16)
    kv = _dot(ck, wkv_ref[...])
    kx_ref[...] = kv[:, 0:1024].astype(BF16)
    vmt_ref[...] = kv[:, 1024:1536].T.astype(BF16)


def _even_proj(x, tab, tab_blocks, w_in, q_norm, w_uq, kv_norm, w_kv, tm):
    m = x.shape[0]
    nt = m // tm
    row = lambda i: (i, 0)
    const = lambda i: (0, 0)
    outs = (
        jax.ShapeDtypeStruct((m, 512), F32),
        jax.ShapeDtypeStruct((m, 256), F32),
        jax.ShapeDtypeStruct((m, 256), F32),
        jax.ShapeDtypeStruct((nt, 1, 256), F32),
        jax.ShapeDtypeStruct((m, MLA_ROW), F32),
        jax.ShapeDtypeStruct((m, 1024), BF16),
        jax.ShapeDtypeStruct((m, 1024), BF16),
        jax.ShapeDtypeStruct((256, m), BF16),
        jax.ShapeDtypeStruct((512, m), BF16),
    )
    return pl.pallas_call(
        _even_proj_kernel,
        out_shape=outs,
        grid=(nt,),
        in_specs=[pl.BlockSpec((tm, D_MODEL), row),
                  pl.BlockSpec((tm, 768), lambda i: (i % tab_blocks, 0)),
                  pl.BlockSpec(w_in.shape, const),
                  pl.BlockSpec(q_norm.shape, const),
                  pl.BlockSpec(w_uq.shape, const),
                  pl.BlockSpec(kv_norm.shape, const),
                  pl.BlockSpec(w_kv.shape, const)],
        out_specs=(pl.BlockSpec((tm, 512), row),
                   pl.BlockSpec((tm, 256), row),
                   pl.BlockSpec((tm, 256), row),
                   pl.BlockSpec((1, 1, 256), lambda i: (i, 0, 0)),
                   pl.BlockSpec((tm, MLA_ROW), row),
                   pl.BlockSpec((tm, 1024), row),
                   pl.BlockSpec((tm, 1024), row),
                   pl.BlockSpec((256, tm), lambda i: (0, i)),
                   pl.BlockSpec((512, tm), lambda i: (0, i))),
        compiler_params=_params(("parallel",)),
        name="even_proj",
    )(x, tab, w_in, q_norm, w_uq, kv_norm, w_kv)


_TN = (((0,), (0,)), ((), ()))


def _dot_tn(a, b):
    return lax.dot_general(a, b, _TN, preferred_element_type=F32)


def _moba_p_kernel(q_ref, k_ref, v_ref, km_ref, o_ref, sel_s, m_s, l_s, acc_s):
    tq = MOBA_BLOCK
    c = pl.program_id(2)
    n_blk = km_ref.shape[0]
    lane = lax.broadcasted_iota(jnp.int32, (tq, LANES), 1)
    lo = lane < A_HEAD_DIM
    q_a = q_ref[:, 0:128]
    q_b = q_ref[:, 128:256]
    q4 = jnp.concatenate([jnp.where(lo, q_a, 0.0), jnp.where(lo, 0.0, q_a),
                          jnp.where(lo, q_b, 0.0), jnp.where(lo, 0.0, q_b)], axis=0)
    gate = _dot_nt(km_ref[...], q4, precision=HIGHEST)
    past = lax.broadcasted_iota(jnp.int32, gate.shape, 0) < c
    sel = _top_k_mask(jnp.where(past, gate, NEG_INF), min(MOBA_TOPK, n_blk), 0)
    sel_s[...] = jnp.where(past, sel, 0.0)

    qb = (q4 * A_SCALE).astype(BF16)

    vrow = lax.broadcasted_iota(jnp.int32, (LANES, tq), 0)
    one_hi = jnp.where(vrow == A_HEAD_DIM, 1.0, 0.0).astype(BF16)
    one_lo = jnp.where(vrow == 0, 1.0, 0.0).astype(BF16)

    def kv_tile(j):
        start = pl.multiple_of(j * tq, tq)
        v = v_ref[:, pl.ds(start, tq)]
        return (k_ref[pl.ds(start, tq), :].astype(BF16),
                jnp.where(vrow < A_HEAD_DIM, v, one_hi), jnp.where(vrow < A_HEAD_DIM, one_lo, v))

    def weighted(p, v_lo, v_hi):
        r = [_dot(v_lo if i % 2 == 0 else v_hi, p[:, i * tq:(i + 1) * tq]) for i in range(4)]
        sums = [r[i][A_HEAD_DIM:A_HEAD_DIM + 1] if i % 2 == 0 else r[i][0:1] for i in range(4)]
        return jnp.concatenate(sums, axis=1), jnp.concatenate(r, axis=1)

    k_own, v_lo, v_hi = kv_tile(c)
    s = _dot_nt(k_own, qb)
    key = lax.broadcasted_iota(jnp.int32, s.shape, 0)
    qpos = lax.broadcasted_iota(jnp.int32, s.shape, 1) & (tq - 1)
    s = jnp.where(key <= qpos, s, NEG_INF)
    m0 = jnp.max(s, axis=0, keepdims=True)
    m_s[...] = m0
    l_s[...], acc_s[...] = weighted(jnp.exp((s - m0).astype(BF16)), v_lo, v_hi)

    def body(j, carry):
        kj, vj_lo, vj_hi = kv_tile(j)
        chosen = sel_s[pl.ds(j, 1), :] > 0.5
        sj = jnp.where(chosen, _dot_nt(kj, qb), NEG_INF)
        m_old = m_s[...]
        m_new = jnp.maximum(m_old, jnp.max(sj, axis=0, keepdims=True))
        a = jnp.exp(m_old - m_new)
        sums, vals = weighted(jnp.exp((sj - m_new).astype(BF16)), vj_lo, vj_hi)
        m_s[...] = m_new
        l_s[...] = a * l_s[...] + sums
        acc_s[...] = a * acc_s[...] + vals
        return carry

    lax.fori_loop(0, c, body, 0)
    o = acc_s[...] / l_s[...]
    top = lax.broadcasted_iota(jnp.int32, (LANES, tq), 0) < A_HEAD_DIM
    o_ref[:, 0:128] = jnp.where(top, o[:, 0:tq], o[:, tq:2 * tq]).T
    o_ref[:, 128:256] = jnp.where(top, o[:, 2 * tq:3 * tq], o[:, 3 * tq:4 * tq]).T


def _moba_prompt(qa, ka, va, kmean, bsz, seq):
    tq = MOBA_BLOCK
    nc = seq // tq
    return pl.pallas_call(
        _moba_p_kernel,
        out_shape=jax.ShapeDtypeStruct((bsz * seq, 512), F32),
        grid=(bsz, 2, nc),
        in_specs=[pl.BlockSpec((tq, 256), lambda b, p, c: (b * nc + c, p)),
                  pl.BlockSpec((seq, 128), lambda b, p, c: (b, p)),
                  pl.BlockSpec((128, seq), lambda b, p, c: (p, b)),
                  pl.BlockSpec((nc, 128), lambda b, p, c: (b, p))],
        out_specs=pl.BlockSpec((tq, 256), lambda b, p, c: (b * nc + c, p)),
        scratch_shapes=[pltpu.VMEM((nc, 4 * tq), F32), pltpu.VMEM((1, 4 * tq), F32),
                        pltpu.VMEM((1, 4 * tq), F32), pltpu.VMEM((128, 4 * tq), F32)],
        compiler_params=_params(("parallel", "parallel", "arbitrary")),
        name="moba_prompt",
    )(qa, ka, va, kmean)


def _mla_p_kernel(q_ref, k_ref, v_ref, o_ref):
    tq = q_ref.shape[0]
    tk = MLA_KV_TILE
    per_q = tq // tk
    c = pl.program_id(2)
    q0 = q_ref[:, 0:128]
    q1 = q_ref[:, 128:256]
    top = lax.broadcasted_iota(jnp.int32, (LANES, tq), 0) < MLA_D_V
    vrow = lax.broadcasted_iota(jnp.int32, (LANES, tk), 0)
    one_hi = jnp.where(vrow == MLA_D_V, 1.0, 0.0).astype(BF16)
    one_lo = jnp.where(vrow == 0, 1.0, 0.0).astype(BF16)

    def scores(j):
        k = k_ref[pl.ds(pl.multiple_of(j * tk, tk), tk), :]
        return _dot_nt(k[:, 0:128], q0), _dot_nt(k[:, 128:256], q1)

    def consume(j, s0, s1, diag_offset, state):
        m0, m1, l0, l1, acc = state
        v = v_ref[:, pl.ds(pl.multiple_of(j * tk, tk), tk)]
        if diag_offset is not None:
            key = lax.broadcasted_iota(jnp.int32, s0.shape, 0) + diag_offset
            qpos = lax.broadcasted_iota(jnp.int32, s0.shape, 1)
            s0 = jnp.where(key <= qpos, s0, NEG_INF)
            s1 = jnp.where(key <= qpos, s1, NEG_INF)
        m0n = jnp.maximum(m0, jnp.max(s0, axis=0, keepdims=True))
        m1n = jnp.maximum(m1, jnp.max(s1, axis=0, keepdims=True))
        a0 = jnp.exp(m0 - m0n)
        a1 = jnp.exp(m1 - m1n)
        r0 = _dot(jnp.where(vrow < MLA_D_V, v, one_hi), jnp.exp((s0 - m0n).astype(BF16)))
        r1 = _dot(jnp.where(vrow < MLA_D_V, one_lo, v), jnp.exp((s1 - m1n).astype(BF16)))
        l0 = a0 * l0 + r0[MLA_D_V:MLA_D_V + 1]
        l1 = a1 * l1 + r1[0:1]
        acc = jnp.where(top, a0, a1) * acc + jnp.where(top, r0, r1)
        return m0n, m1n, l0, l1, acc

    neg = jnp.full((1, tq), NEG_INF, F32)
    zero = jnp.zeros((1, tq), F32)
    state = (neg, neg, zero, zero, jnp.zeros((LANES, tq), F32))
    for d in range(per_q):
        state = consume(c * per_q + d, *scores(c * per_q + d), d * tk, state)

    n_past = c * per_q

    def body(j, carry):
        s0, s1, st = carry
        nxt = scores(jnp.minimum(j + 1, n_past - 1))
        return nxt + (consume(j, s0, s1, None, st),)

    _, _, (m0, m1, l0, l1, acc) = lax.fori_loop(0, n_past, body, scores(0) + (state,))
    o_ref[...] = (acc / jnp.where(top, l0, l1)).T


def _mla_prompt(qm, kx, vm, bsz, seq, tq=512):
    nc = seq // tq
    return pl.pallas_call(
        _mla_p_kernel,
        out_shape=jax.ShapeDtypeStruct((bsz * seq, 512), F32),
        grid=(bsz, B_HEADS // 2, nc),
        in_specs=[pl.BlockSpec((tq, 256), lambda b, h, c: (b * nc + c, h)),
                  pl.BlockSpec((seq, 256), lambda b, h, c: (b, h)),
                  pl.BlockSpec((128, seq), lambda b, h, c: (h, b))],
        out_specs=pl.BlockSpec((tq, 128), lambda b, h, c: (b * nc + c, h)),
        compiler_params=_params(("parallel", "parallel", "arbitrary")),
        name="mla_prompt",
    )(qm, kx, vm)


def _page_copies(pt_ref, cache_hbm, buf, sem, bb, slot, n_pages, page_base):
    copies = []
    for j in range(n_pages):
        pg = pt_ref[bb, j] + page_base
        copies.append(pltpu.make_async_copy(
            cache_hbm.at[pg], buf.at[slot, :, pl.ds(j * PAGE_SIZE, PAGE_SIZE)], sem.at[slot]))
    return copies


def _prefetch_pages(pt_ref, caches, bufs, sems, n_pages, page_base):
    b = pl.program_id(0)
    nb = pl.num_programs(0)
    slot = b % 2

    def start(bb, sl):
        for cache, buf, sem in zip(caches, bufs, sems):
            for cp in _page_copies(pt_ref, cache, buf, sem, bb, sl, n_pages, page_base):
                cp.start()

    @pl.when(b == 0)
    def _():
        start(0, 0)

    @pl.when(b + 1 < nb)
    def _():
        start(b + 1, 1 - slot)

    for cache, buf, sem in zip(caches, bufs, sems):
        for cp in _page_copies(pt_ref, cache, buf, sem, b, slot, n_pages, page_base):
            cp.wait()
    return slot


def _moba_s_kernel(page_base, pt_ref, q_ref, kn_ref, vn_ref, kc_hbm, vc_hbm, o_ref,
                   kbuf, vbuf, s_s, ksem, vsem):
    n_keys = kbuf.shape[2]
    n_pages = n_keys // PAGE_SIZE
    n_blk = n_keys // MOBA_BLOCK
    chunk = 1024
    n_chunk = n_keys // chunk
    slot = _prefetch_pages(pt_ref, (kc_hbm, vc_hbm), (kbuf, vbuf), (ksem, vsem), n_pages, page_base)

    q = q_ref[0]
    qb = (q * A_SCALE).astype(BF16)
    feat = kbuf.shape[1]
    blk_lane = lax.broadcasted_iota(jnp.int32, (feat, LANES), 1)
    km_t = jnp.zeros((feat, LANES), F32)
    for n in range(n_blk):
        mean_n = jnp.mean(kbuf[slot, :, n * MOBA_BLOCK:(n + 1) * MOBA_BLOCK], axis=1, keepdims=True)
        km_t = jnp.where(blk_lane == n, mean_n, km_t)
    for ci in range(n_chunk):
        kc = kbuf[slot, :, ci * chunk:(ci + 1) * chunk]
        s_s[:, ci * chunk:(ci + 1) * chunk] = _dot(qb, kc.astype(BF16))
    gate = jnp.dot(q, km_t, precision=HIGHEST, preferred_element_type=F32)[:, 0:n_blk]
    sel = _top_k_mask(gate, min(MOBA_TOPK, n_blk), 1)
    key_blk = lax.broadcasted_iota(jnp.int32, (n_blk, n_keys), 1) >> 8
    expand = jnp.where(key_blk == lax.broadcasted_iota(jnp.int32, (n_blk, n_keys), 0), 1.0, 0.0)
    chosen = _dot(sel.astype(BF16), expand.astype(BF16)) > 0.5
    s = jnp.where(chosen, s_s[...], NEG_INF)

    s_new = _dot_nt(qb, kn_ref[0].astype(BF16))
    tok = (lax.broadcasted_iota(jnp.int32, s_new.shape, 0) & 7) >> 1
    s_new = jnp.where(lax.broadcasted_iota(jnp.int32, s_new.shape, 1) <= tok, s_new, NEG_INF)
    m = jnp.maximum(jnp.max(s, axis=1, keepdims=True), jnp.max(s_new, axis=1, keepdims=True))
    p = jnp.exp(s - m)
    p_new = jnp.exp(s_new - m)
    l = jnp.sum(p, axis=1, keepdims=True) + jnp.sum(p_new, axis=1, keepdims=True)
    acc = _dot(p_new.astype(BF16), vn_ref[0].astype(BF16))
    pb = p.astype(BF16)
    for ci in range(n_chunk):
        vc = vbuf[slot, :, ci * chunk:(ci + 1) * chunk].astype(BF16)
        acc = acc + _dot_nt(pb[:, ci * chunk:(ci + 1) * chunk], vc)
    o = acc / l
    lane_head = lax.broadcasted_iota(jnp.int32, (8, o.shape[1]), 1) >> 6
    out = jnp.zeros((8, o.shape[1]), F32)
    for k in range(A_KV_HEADS):
        out = out + jnp.where(lane_head == k, o[8 * k:8 * k + 8], 0.0)
    o_ref[0] = out


def _moba_sample(page_table, q_rows, k_new, v_new, cache_k, cache_v, page_base):
    dbs, n_pages = page_table.shape
    n_keys = n_pages * PAGE_SIZE
    grid_spec = pltpu.PrefetchScalarGridSpec(
        num_scalar_prefetch=1,
        grid=(dbs,),
        in_specs=[pl.BlockSpec((1,) + q_rows.shape[1:], lambda b, pt: (b, 0, 0)),
                  pl.BlockSpec((1,) + k_new.shape[1:], lambda b, pt: (b, 0, 0)),
                  pl.BlockSpec((1,) + v_new.shape[1:], lambda b, pt: (b, 0, 0)),
                  pl.BlockSpec(memory_space=pl.ANY),
                  pl.BlockSpec(memory_space=pl.ANY)],
        out_specs=pl.BlockSpec((1, 8, 256), lambda b, pt: (b, 0, 0)),
        scratch_shapes=[pltpu.VMEM((2, 256, n_keys), F32), pltpu.VMEM((2, 256, n_keys), F32),
                        pltpu.VMEM((q_rows.shape[1], n_keys), F32),
                        pltpu.SemaphoreType.DMA((2,)), pltpu.SemaphoreType.DMA((2,))],
    )
    return pl.pallas_call(
        functools.partial(_moba_s_kernel, page_base),
        out_shape=jax.ShapeDtypeStruct((dbs, 8, 256), F32),
        grid_spec=grid_spec,
        compiler_params=_params(("arbitrary",), 60 * 1024 * 1024),
        name="moba_sample",
    )(page_table, q_rows, k_new, v_new, cache_k, cache_v)


def _mla_s_kernel(page_base, pt_ref, q_ref, new_ref, c_hbm, o_ref, cbuf, s_s, csem):
    n_keys = cbuf.shape[2]
    n_pages = n_keys // PAGE_SIZE
    chunk = 1024
    n_chunk = n_keys // chunk
    slot = _prefetch_pages(pt_ref, (c_hbm,), (cbuf,), (csem,), n_pages, page_base)

    q = q_ref[0][:, 0:MLA_ROW]
    for ci in range(n_chunk):
        pc = cbuf[slot, :, ci * chunk:(ci + 1) * chunk].astype(BF16)
        s_s[:, ci * chunk:(ci + 1) * chunk] = _dot(q, pc)
    s = s_s[...]
    new = new_ref[0]
    s_new = _dot_nt(q, new.astype(BF16))
    tok = lax.broadcasted_iota(jnp.int32, s_new.shape, 0) >> 3
    s_new = jnp.where(lax.broadcasted_iota(jnp.int32, s_new.shape, 1) <= tok, s_new, NEG_INF)
    m = jnp.maximum(jnp.max(s, axis=1, keepdims=True), jnp.max(s_new, axis=1, keepdims=True))
    p = jnp.exp(s - m)
    p_new = jnp.exp(s_new - m)
    l = jnp.sum(p, axis=1, keepdims=True) + jnp.sum(p_new, axis=1, keepdims=True)
    acc = _dot(p_new.astype(BF16), new[:, 0:MLA_KV_RANK].astype(BF16))
    pb = p.astype(BF16)
    for ci in range(n_chunk):
        cc = cbuf[slot, 0:MLA_KV_RANK, ci * chunk:(ci + 1) * chunk].astype(BF16)
        acc = acc + _dot_nt(pb[:, ci * chunk:(ci + 1) * chunk], cc)
    o_ref[0] = acc / l


def _mla_sample(page_table, q_rows, new_rows, cache, page_base):
    dbs, n_pages = page_table.shape
    n_keys = n_pages * PAGE_SIZE
    grid_spec = pltpu.PrefetchScalarGridSpec(
        num_scalar_prefetch=1,
        grid=(dbs,),
        in_specs=[pl.BlockSpec((1,) + q_rows.shape[1:], lambda b, pt: (b, 0, 0)),
                  pl.BlockSpec((1,) + new_rows.shape[1:], lambda b, pt: (b, 0, 0)),
                  pl.BlockSpec(memory_space=pl.ANY)],
        out_specs=pl.BlockSpec((1, q_rows.shape[1], MLA_KV_RANK), lambda b, pt: (b, 0, 0)),
        scratch_shapes=[pltpu.VMEM((2, MLA_ROW, n_keys), F32),
                        pltpu.VMEM((q_rows.shape[1], n_keys), F32),
                        pltpu.SemaphoreType.DMA((2,))],
    )
    return pl.pallas_call(
        functools.partial(_mla_s_kernel, page_base),
        out_shape=jax.ShapeDtypeStruct((dbs, q_rows.shape[1], MLA_KV_RANK), F32),
        grid_spec=grid_spec,
        compiler_params=_params(("arbitrary",), 60 * 1024 * 1024),
        name="mla_sample",
    )(page_table, q_rows, new_rows, cache)


def _odd_proj_kernel(dils, x_ref, tab_ref, w_ref, o0_ref, o1_ref, o2_ref, scr):
    tm = x_ref.shape[0]
    lane = lax.broadcasted_iota(jnp.int32, (tm, LANES), 1)
    cos, sin = tab_ref[:, 0:128], tab_ref[:, 128:256]
    xb = x_ref[...].astype(BF16)
    for g, (o_ref, d) in enumerate(zip((o0_ref, o1_ref, o2_ref), dils)):
        y = _dot(xb, w_ref[:, g * 1536:(g + 1) * 1536])
        for j in range(12):
            cols = slice(j * 128, (j + 1) * 128)
            blk = y[:, cols]
            if j < 8:
                blk = _rope_block(blk, cos, sin, 64, lane)
            if d == 1:
                o_ref[0, 0, :, cols] = blk
            else:
                scr[...] = blk
                for r in range(d):
                    o_ref[0, r, :, cols] = scr[pl.ds(r, tm // d, stride=d), :]


def _odd_proj(x, tab, tab_blocks, w, tm, bsz, dils):
    m = x.shape[0]
    tpb = m // bsz // tm
    outs = tuple(jax.ShapeDtypeStruct((bsz, d, m // bsz // d, 1536), F32) for d in dils)
    return pl.pallas_call(
        functools.partial(_odd_proj_kernel, dils),
        out_shape=outs,
        grid=(m // tm,),
        in_specs=[pl.BlockSpec((tm, D_MODEL), lambda i: (i, 0)),
                  pl.BlockSpec((tm, 256), lambda i: (i % tab_blocks, 0)),
                  pl.BlockSpec(w.shape, lambda i: (0, 0))],
        out_specs=tuple(pl.BlockSpec((1, d, tm // d, 1536), lambda i: (i // tpb, 0, i % tpb, 0)) for d in dils),
        scratch_shapes=[pltpu.VMEM((tm, LANES), F32)],
        compiler_params=_params(("parallel",)),
        name="odd_proj",
    )(x, tab, w)


def _dil_p_kernel(q_ref, kp_ref, kc_ref, vp_ref, vc_ref, o_ref, lse_ref):
    tq = q_ref.shape[2]
    win = SWA_KEYS
    mt = pl.program_id(2)
    i = lax.broadcasted_iota(jnp.int32, (tq, win + tq), 0)
    kk = lax.broadcasted_iota(jnp.int32, (tq, win + tq), 1)
    first_ok = jnp.where(mt > 0, 0, win)
    band = (kk >= i) & (kk <= i + win) & (kk >= first_ok)
    lane = lax.broadcasted_iota(jnp.int32, (tq, LANES), 1)
    stats = jnp.zeros((tq, LANES), F32)
    for h in range(C_HEADS):
        hs = slice(h * 128, (h + 1) * 128)
        q = (q_ref[0, 0, :, hs] * C_SCALE).astype(BF16)
        k = jnp.concatenate([kp_ref[0, 0, :, hs], kc_ref[0, 0, :, hs]], axis=0).astype(BF16)
        v = jnp.concatenate([vp_ref[0, 0, :, hs], vc_ref[0, 0, :, hs]], axis=0).astype(BF16)
        s = jnp.where(band, _dot_nt(q, k), NEG_INF)
        m = jnp.max(s, axis=1, keepdims=True)
        p = jnp.exp(s - m)
        l = jnp.sum(p, axis=1, keepdims=True)
        o_ref[0, 0, :, hs] = _dot(p.astype(BF16), v) / l
        stats = jnp.where(lane == h, m + jnp.log(l), stats)
    lse_ref[0, 0] = stats


def _dilated_prompt(hv):
    bsz, dil, sd, _ = hv.shape
    win = SWA_KEYS
    tq = min(2 * win, sd)
    nt = sd // tq
    prev = lambda mt: jnp.maximum(mt * (tq // win) - 1, 0)
    return pl.pallas_call(
        _dil_p_kernel,
        out_shape=(jax.ShapeDtypeStruct((bsz, dil, sd, 512), F32),
                   jax.ShapeDtypeStruct((bsz, dil, sd, 128), F32)),
        grid=(bsz, dil, nt),
        in_specs=[pl.BlockSpec((1, 1, tq, 512), lambda b, r, mt: (b, r, mt, 0)),
                  pl.BlockSpec((1, 1, win, 512), lambda b, r, mt: (b, r, prev(mt), 1)),
                  pl.BlockSpec((1, 1, tq, 512), lambda b, r, mt: (b, r, mt, 1)),
                  pl.BlockSpec((1, 1, win, 512), lambda b, r, mt: (b, r, prev(mt), 2)),
                  pl.BlockSpec((1, 1, tq, 512), lambda b, r, mt: (b, r, mt, 2))],
        out_specs=(pl.BlockSpec((1, 1, tq, 512), lambda b, r, mt: (b, r, mt, 0)),
                   pl.BlockSpec((1, 1, tq, 128), lambda b, r, mt: (b, r, mt, 0))),
        compiler_params=_params(("parallel", "parallel", "arbitrary")),
        name="dilated_prompt",
    )(hv, hv, hv, hv, hv)


def _dil_s_kernel(q0_ref, q1_ref, q2_ref, n0_ref, n1_ref, n2_ref, c0_ref, c1_ref, c2_ref,
                  o0_ref, o1_ref, o2_ref, l0_ref, l1_ref, l2_ref):
    q_refs = (q0_ref, q1_ref, q2_ref)
    n_refs = (n0_ref, n1_ref, n2_ref)
    c_refs = (c0_ref, c1_ref, c2_ref)
    o_refs = (o0_ref, o1_ref, o2_ref)
    l_refs = (l0_ref, l1_ref, l2_ref)
    for g in range(N_SWA):
        q = (q_refs[g][0] * C_SCALE).astype(BF16)
        cache = c_refs[g][0]
        n_old = cache.shape[0] * cache.shape[1]
        rows = jnp.concatenate([cache.reshape(n_old, C_HEAD_DIM), n_refs[g][0]], axis=0).astype(BF16)
        s = _dot_nt(q, rows)
        col = lax.broadcasted_iota(jnp.int32, s.shape, 1)
        row = lax.broadcasted_iota(jnp.int32, s.shape, 0)
        tok, head = row >> 2, row & 3
        new_col = col - n_old
        is_old = col < n_old
        if g == 0:
            ok_old = ((col & 7) == head) & ((col >> 3) >= tok)
            ok_new = ((new_col & 7) == head) & ((new_col >> 3) <= tok)
        else:
            ok_old = (col & 31) == tok * 8 + head
            ok_new = new_col == tok * 8 + head
        s = jnp.where((is_old & ok_old) | ((col >= n_old) & ok_new), s, NEG_INF)
        m = jnp.max(s, axis=1, keepdims=True)
        p = jnp.exp(s - m)
        l = jnp.sum(p, axis=1, keepdims=True)
        pv = pltpu.roll(p, 4, 1).astype(BF16)
        o_refs[g][0] = _dot(pv, rows) / l
        l_refs[g][0] = jnp.broadcast_to(m + jnp.log(l), (q.shape[0], LANES))


def _dilated_sample(q_list, new_list, caches, row_base):
    dbs = q_list[0].shape[0]
    sds = jax.ShapeDtypeStruct((dbs, 16, LANES), F32)
    idx = lambda b: (b, 0, 0)
    cidx = lambda b: (row_base + b, 0, 0, 0)
    cache_rows = (8, 32, 32)
    outs = pl.pallas_call(
        _dil_s_kernel,
        out_shape=(sds,) * 6,
        grid=(dbs,),
        in_specs=[pl.BlockSpec((1, 16, LANES), idx)] * 3 + [pl.BlockSpec((1, LANES, LANES), idx)] * 3
        + [pl.BlockSpec((1, SWA_KEYS, r, C_HEAD_DIM), cidx) for r in cache_rows],
        out_specs=(pl.BlockSpec((1, 16, LANES), idx),) * 6,
        compiler_params=_params(("parallel",)),
        name="dilated_sample",
    )(*q_list, *new_list, *caches)
    return outs[:3], outs[3:]


def _odd_out_kernel(o0_ref, o1_ref, o2_ref, l0_ref, l1_ref, l2_ref, w_ref, r_ref, g_ref, b_ref, out_ref, scr):
    tm = r_ref.shape[0]

    def token_major(ref, cols):
        d = ref.shape[1]
        if d == 1:
            return ref[0, 0, :, cols]
        for r in range(d):
            scr[pl.ds(r, tm // d, stride=d), :] = ref[0, r, :, cols]
        return scr[...]

    lses = [token_major(ref, slice(0, LANES)) for ref in (l0_ref, l1_ref, l2_ref)]
    parts = []
    for h in range(C_HEADS):
        hs = slice(h * 128, (h + 1) * 128)
        a0, a1, a2 = (ls[:, h:h + 1] for ls in lses)
        mx = jnp.maximum(jnp.maximum(a0, a1), a2)
        e0, e1, e2 = jnp.exp(a0 - mx), jnp.exp(a1 - mx), jnp.exp(a2 - mx)
        den = e0 + e1 + e2
        parts.append((e0 / den) * token_major(o0_ref, hs) + (e1 / den) * token_major(o1_ref, hs)
                     + (e2 / den) * token_major(o2_ref, hs))
    comb = jnp.concatenate(parts, axis=1).astype(BF16)
    out_ref[...] = _layer_norm(ALPHA * r_ref[...] + _dot(comb, w_ref[...]), g_ref[...], b_ref[...])


def _odd_out(outs, lses, w, resid, g, b, tm):
    m = resid.shape[0]
    bsz = outs[0].shape[0]
    tpb = m // bsz // tm
    row = lambda i: (i, 0)
    const = lambda i: (0, 0)
    res = lambda a: pl.BlockSpec((1, a.shape[1], tm // a.shape[1], a.shape[3]), lambda i: (i // tpb, 0, i % tpb, 0))
    return pl.pallas_call(
        _odd_out_kernel,
        out_shape=jax.ShapeDtypeStruct((m, D_MODEL), F32),
        grid=(m // tm,),
        in_specs=[res(a) for a in outs] + [res(a) for a in lses]
        + [pl.BlockSpec(w.shape, const), pl.BlockSpec((tm, D_MODEL), row),
           pl.BlockSpec((1, D_MODEL), const), pl.BlockSpec((1, D_MODEL), const)],
        out_specs=pl.BlockSpec((tm, D_MODEL), row),
        scratch_shapes=[pltpu.VMEM((tm, LANES), F32)],
        compiler_params=_params(("parallel",)),
        name="odd_out",
    )(*outs, *lses, w, resid, g, b)


def _mem_heads(q, kv_ref_val):
    outs = []
    for h in range(MEM_HEADS):
        k = kv_ref_val[:, h * 128:(h + 1) * 128]
        v = kv_ref_val[:, 512 + h * 128:512 + (h + 1) * 128]
        s = _dot_nt(q[:, h * 128:(h + 1) * 128].astype(BF16), k)
        m = jnp.max(s, axis=1, keepdims=True)
        p = jnp.exp(s - m)
        l = jnp.sum(p, axis=1, keepdims=True)
        outs.append(_dot(p.astype(BF16), v) / l)
    return jnp.concatenate(outs, axis=1)


def _mem_p_kernel(x_ref, kv_ref, wq_ref, wo_ref, g_ref, b_ref, o_ref):
    x = x_ref[...]
    q = _dot(x.astype(BF16), wq_ref[...]) * MEM_SCALE
    att = _mem_heads(q, kv_ref[0])
    o_ref[...] = _layer_norm(ALPHA * x + _dot(att.astype(BF16), wo_ref[...]), g_ref[...], b_ref[...])


def _mem_prompt(x, kv, wq, wo, g, b, tiles_per_batch, tm):
    m = x.shape[0]
    row = lambda i: (i, 0)
    const = lambda i: (0, 0)
    return pl.pallas_call(
        _mem_p_kernel,
        out_shape=jax.ShapeDtypeStruct((m, D_MODEL), F32),
        grid=(m // tm,),
        in_specs=[pl.BlockSpec((tm, D_MODEL), row),
                  pl.BlockSpec((1, N_MEM, 1024), lambda i: (i // tiles_per_batch, 0, 0)),
                  pl.BlockSpec(wq.shape, const), pl.BlockSpec(wo.shape, const),
                  pl.BlockSpec((1, D_MODEL), const), pl.BlockSpec((1, D_MODEL), const)],
        out_specs=pl.BlockSpec((tm, D_MODEL), row),
        compiler_params=_params(("parallel",)),
        name="mem_prompt",
    )(x, kv, wq, wo, g, b)


def _mem_s_kernel(q_ref, kv_ref, o_ref):
    for i in range(q_ref.shape[0]):
        rows = kv_ref[i].astype(BF16)
        q = (q_ref[i] * MEM_SCALE).astype(BF16)
        s = _dot_nt(q, rows)
        col = lax.broadcasted_iota(jnp.int32, s.shape, 1)
        head = lax.broadcasted_iota(jnp.int32, s.shape, 0) & 3
        s = jnp.where((col & 7) == head, s, NEG_INF)
        m = jnp.max(s, axis=1, keepdims=True)
        p = jnp.exp(s - m)
        l = jnp.sum(p, axis=1, keepdims=True)
        pv = pltpu.roll(p, 4, 1).astype(BF16)
        o_ref[i] = _dot(pv, rows) / l


def _mem_sample_attn(q, kv, row_base):
    dbs = q.shape[0]
    per_step = MEM_SAMPLE_PER_STEP
    base_blk = row_base // per_step
    idx = lambda b: (b, 0, 0)
    return pl.pallas_call(
        _mem_s_kernel,
        out_shape=jax.ShapeDtypeStruct(q.shape, F32),
        grid=(dbs // per_step,),
        in_specs=[pl.BlockSpec((per_step,) + q.shape[1:], idx),
                  pl.BlockSpec((per_step,) + kv.shape[1:], lambda b: (base_blk + b, 0, 0))],
        out_specs=pl.BlockSpec((per_step,) + q.shape[1:], idx),
        compiler_params=_params(("parallel",)),
        name="mem_sample",
    )(q, kv)


def _router_kernel(x_ref, wr_ref, rb_ref, gate_ref):
    tm = x_ref.shape[0]
    per_grp = N_EXPERTS // N_EXPERT_GROUPS
    logits = _dot_nt(wr_ref[...], x_ref[...], precision=HIGHEST)
    s = _sigmoid(logits)
    sb = s + rb_ref[...]
    sub = lax.broadcasted_iota(jnp.int32, (per_grp, tm), 0).astype(F32)
    slabs, gscore = [], []
    for gi in range(N_EXPERT_GROUPS):
        slab = sb[gi * per_grp:(gi + 1) * per_grp, :]
        m1 = jnp.max(slab, axis=0, keepdims=True)
        f1 = jnp.min(jnp.where(slab == m1, sub, float(per_grp)), axis=0, keepdims=True)
        m2 = jnp.max(jnp.where(sub == f1, -jnp.inf, slab), axis=0, keepdims=True)
        slabs.append(slab)
        gscore.append(m1 + m2)
    masked = []
    for gi in range(N_EXPERT_GROUPS):
        beaten = jnp.zeros((1, tm), F32)
        for gj in range(N_EXPERT_GROUPS):
            if gj < gi:
                beaten = beaten + jnp.where(gscore[gj] >= gscore[gi], 1.0, 0.0)
            elif gj > gi:
                beaten = beaten + jnp.where(gscore[gj] > gscore[gi], 1.0, 0.0)
        masked.append(jnp.where(beaten < TOPK_GROUPS - 0.5, slabs[gi], NEG_INF))
    esel = _top_k_mask(jnp.concatenate(masked, axis=0), MOE_TOPK, 0)
    w = s * esel
    w = w / jnp.sum(w, axis=0, keepdims=True) * ROUTED_SCALE
    wpad = jnp.concatenate([w, jnp.zeros((LANES - N_EXPERTS, tm), F32)], axis=0)
    gate_ref[...] = wpad.T


def _router(x, wr_t, rb, tm):
    m = x.shape[0]
    tm = min(tm, m)
    return pl.pallas_call(
        _router_kernel,
        out_shape=jax.ShapeDtypeStruct((m, LANES), F32),
        grid=(m // tm,),
        in_specs=[pl.BlockSpec((tm, D_MODEL), lambda i: (i, 0)),
                  pl.BlockSpec(wr_t.shape, lambda i: (0, 0)),
                  pl.BlockSpec(rb.shape, lambda i: (0, 0))],
        out_specs=pl.BlockSpec((tm, LANES), lambda i: (i, 0)),
        compiler_params=_params(("parallel",)),
        name="router",
    )(x, wr_t, rb)


def _moe_kernel(x_ref, gate_ref, wg_ref, wu_ref, wd_ref, sg_ref, su_ref, sd_ref, g_ref, b_ref,
                o_ref, xb_s):
    e = pl.program_id(1)

    @pl.when(e == 0)
    def _():
        xb0 = x_ref[...].astype(BF16)
        xb_s[...] = xb0
        hg = _dot(xb0, sg_ref[...])
        hs = hg * _sigmoid(hg) * _dot(xb0, su_ref[...])
        o_ref[...] = _dot(hs.astype(BF16), sd_ref[...])

    xb = xb_s[...]
    hg = _dot(xb, wg_ref[0, 0].astype(BF16))
    h = hg * _sigmoid(hg) * _dot(xb, wu_ref[0, 0].astype(BF16))
    gate = gate_ref[...]
    lane = lax.broadcasted_iota(jnp.int32, gate.shape, 1)
    gcol = jnp.sum(jnp.where(lane == e, gate, 0.0), axis=1, keepdims=True)
    o_ref[...] += _dot((h * gcol).astype(BF16), wd_ref[0, 0].astype(BF16))

    @pl.when(e == pl.num_programs(1) - 1)
    def _():
        o_ref[...] = _layer_norm(ALPHA * x_ref[...] + o_ref[...], g_ref[...], b_ref[...])


def _moe(x, gate, layer, wg, wu, wd, sg, su, sd, g, b, tm):
    m = x.shape[0]
    tm = min(tm, m)
    row = lambda i, e: (i, 0)
    const = lambda i, e: (0, 0)
    expert = lambda i, e: (layer, e, 0, 0)
    return pl.pallas_call(
        _moe_kernel,
        out_shape=jax.ShapeDtypeStruct((m, D_MODEL), F32),
        grid=(m // tm, N_EXPERTS),
        in_specs=[pl.BlockSpec((tm, D_MODEL), row),
                  pl.BlockSpec((tm, LANES), row),
                  pl.BlockSpec((1, 1, D_MODEL, D_EXPERT), expert),
                  pl.BlockSpec((1, 1, D_MODEL, D_EXPERT), expert),
                  pl.BlockSpec((1, 1, D_EXPERT, D_MODEL), expert),
                  pl.BlockSpec(sg.shape, const), pl.BlockSpec(su.shape, const),
                  pl.BlockSpec(sd.shape, const),
                  pl.BlockSpec((1, D_MODEL), const), pl.BlockSpec((1, D_MODEL), const)],
        out_specs=pl.BlockSpec((tm, D_MODEL), row),
        scratch_shapes=[pltpu.VMEM((tm, D_MODEL), BF16)],
        compiler_params=_params(("parallel", "arbitrary")),
        name="moe",
    )(x, gate, wg, wu, wd, sg, su, sd, g, b)


def _rope_cs(pos, half):
    inv = ROPE_THETA ** (-jnp.arange(half, dtype=F32) / half)
    ang = pos.astype(F32)[:, None] * inv[None, :]
    return jnp.cos(ang), jnp.sin(ang)


def _even_table(pos):
    n = pos.shape[0]
    c32, s32 = _rope_cs(pos, 32)
    c16, s16 = _rope_cs(pos, 16)
    one = lambda w: jnp.ones((n, w), F32)
    zero = lambda w: jnp.zeros((n, w), F32)
    cos_a = jnp.tile(c32, (1, 4))
    sin_a = jnp.tile(jnp.concatenate([-s32, s32], axis=1), (1, 2))
    cos_b = jnp.concatenate([one(64), c16, c16, one(32)], axis=1)
    sin_b = jnp.concatenate([zero(64), -s16, s16, zero(32)], axis=1)
    cos_c = jnp.concatenate([c16, c16, one(96)], axis=1)
    sin_c = jnp.concatenate([-s16, s16, zero(96)], axis=1)
    return jnp.concatenate([cos_a, sin_a, cos_b, sin_b, cos_c, sin_c], axis=1)


def _odd_table(pos):
    c64, s64 = _rope_cs(pos, 64)
    return jnp.concatenate([c64, c64, -s64, s64], axis=1)


_MOBA_HEAD_PERM = (0, 2, 1, 3, 4, 6, 5, 7)


def _even_weights(w_in, w_uq, w_uk, w_uv, w_out):
    hp = np.asarray(_MOBA_HEAD_PERM)
    qcols = (hp[:, None] * A_HEAD_DIM + np.arange(A_HEAD_DIM)[None, :]).reshape(-1)
    w_q = w_in[:, :512][:, qcols]
    w_in_p = jnp.concatenate(
        [w_q, w_in[:, 512:], jnp.zeros((D_MODEL, EVEN_W - w_in.shape[1]), F32)], axis=1).astype(BF16)
    uq = w_uq.reshape(MLA_Q_RANK, B_HEADS, MLA_D_NOPE + MLA_D_ROPE)
    w_uq_p = jnp.pad(uq, ((0, 0), (0, 0), (0, 32))).reshape(MLA_Q_RANK, B_HEADS * 128).astype(BF16)
    eye = jnp.eye(MLA_D_ROPE, dtype=F32)
    k_top = jnp.pad(w_uk, ((0, 0), (0, 0), (0, 64)))
    k_mid = jnp.broadcast_to(jnp.pad(eye, ((0, 0), (64, 32)))[:, None, :], (MLA_D_ROPE, B_HEADS, 128))
    k_all = jnp.concatenate([k_top, k_mid, jnp.zeros((96, B_HEADS, 128), F32)], axis=0).reshape(384, 1024)
    v_all = jnp.concatenate([w_uv.reshape(MLA_KV_RANK, 512), jnp.zeros((128, 512), F32)], axis=0)
    w_kv = jnp.concatenate([k_all, v_all], axis=1).astype(BF16)
    blocks = []
    for h in range(B_HEADS):
        blk = jnp.zeros((128, 384), F32)
        blk = blk.at[0:64, 0:256].set(w_uk[:, h, :].T)
        blk = blk.at[64:96, 256:288].set(eye)
        blocks.append(blk)
    w_abs = jax.scipy.linalg.block_diag(*blocks).astype(BF16)
    w_uv_bd = jax.scipy.linalg.block_diag(*[w_uv[:, h, :] for h in range(B_HEADS)]).astype(BF16)
    rows = (hp[:, None] * A_HEAD_DIM + np.arange(A_HEAD_DIM)[None, :]).reshape(-1)
    w_out_moba_p = w_out[:512][rows].astype(BF16)
    w_out_moba = w_out[:512].astype(BF16)
    w_out_mla = w_out[512:].astype(BF16)
    return w_in_p, w_uq_p, w_kv, w_abs, w_uv_bd, w_out_moba_p, w_out_moba, w_out_mla


def kernel(x_prompt, x_sample, mem_prompt, cache_moba_k, cache_moba_v, cache_mla, cache_swa0_kv, cache_swa1_kv, cache_swa2_kv, cache_mem_kv, page_table, ln_gain, ln_bias, w_in_even, w_out_even, mla_q_norm, mla_w_uq, mla_kv_norm, mla_w_uk, mla_w_uv, w_in_odd, w_out_odd, w_mem_q, w_mem_kv, w_mem_o, w_router, router_bias, w_exp_gate, w_exp_up, w_exp_down, w_sh_gate, w_sh_up, w_sh_down):
    bp, sp, _ = x_prompt.shape
    bs, ss, _ = x_sample.shape
    n_p = bp * sp
    n_s = bs * ss
    tm = 256
    n_pool = cache_moba_k.shape[1]
    swa_caches = (cache_swa0_kv, cache_swa1_kv, cache_swa2_kv)

    xp = x_prompt.reshape(n_p, D_MODEL)
    xs = x_sample.reshape(n_s, D_MODEL)
    pos_p = jnp.arange(sp)
    pos_s = jnp.tile(PAST_LEN + jnp.arange(ss), n_s // ss)
    tab_even_p, tab_even_s = _even_table(pos_p), _even_table(pos_s)
    tab_odd_p, tab_odd_s = _odd_table(pos_p), _odd_table(pos_s)
    mem2 = mem_prompt.reshape(bp * N_MEM, D_MODEL)

    ck_pages = jnp.transpose(cache_moba_k, (0, 1, 3, 4, 2)).reshape(-1, A_KV_HEADS * A_HEAD_DIM, PAGE_SIZE)
    cv_pages = jnp.transpose(cache_moba_v, (0, 1, 3, 4, 2)).reshape(-1, A_KV_HEADS * A_HEAD_DIM, PAGE_SIZE)
    cm_pages = jnp.transpose(cache_mla, (0, 1, 3, 2)).reshape(-1, MLA_ROW, PAGE_SIZE)
    swa_rows = [c.reshape(-1, SWA_KEYS, dil * 2 * C_HEADS, C_HEAD_DIM)
                for c, (win, dil) in zip(swa_caches, SWA_CONFIGS)]
    mem_rows = cache_mem_kv.reshape(-1, N_MEM * 2 * MEM_HEADS, MEM_HEAD_DIM)

    moba_k_p, moba_v_p, mla_p, mem_p = [], [], [], []
    moba_k_s, moba_v_s, mla_s = [], [], []
    swa_p = [[] for _ in range(N_SWA)]
    swa_s = [[] for _ in range(N_SWA)]

    for l in range(DEPTH):
        ln_g = [ln_gain[l, i].reshape(1, D_MODEL) for i in range(3)]
        ln_b = [ln_bias[l, i].reshape(1, D_MODEL) for i in range(3)]
        if l % 2 == 0:
            e = l // 2
            (w_in_p, w_uq_p, w_kv, w_abs, w_uv_bd, w_out_moba_p, w_out_moba, w_out_mla) = _even_weights(
                w_in_even[e], mla_w_uq[e], mla_w_uk[e], mla_w_uv[e], w_out_even[e])
            qn = mla_q_norm[e].reshape(1, MLA_Q_RANK)
            kvn = mla_kv_norm[e].reshape(1, MLA_KV_RANK)
            qa, ka, va, km, mla, qm, kx, vat, vmt = _even_proj(
                xp, tab_even_p, sp // tm, w_in_p, qn, w_uq_p, kvn, w_kv, tm)
            o_moba = _moba_prompt(qa, ka, vat, km.reshape(n_p // MOBA_BLOCK, 256), bp, sp)
            o_mla = _mla_prompt(qm, kx, vmt, bp, sp)
            xp = _mm_res_ln([o_moba, o_mla], [w_out_moba_p, w_out_mla], xp, ln_g[0], ln_b[0], tm)
            moba_k_p.append(ka.reshape(bp, sp, A_KV_HEADS, A_HEAD_DIM))
            moba_v_p.append(va.reshape(bp, sp, A_KV_HEADS, A_HEAD_DIM))
            mla_p.append(mla.reshape(bp, sp, MLA_ROW))
            qa, ka, va, km, mla, qm, kx, vat, vmt = _even_proj(
                xs, tab_even_s, n_s // tm, w_in_p, qn, w_uq_p, kvn, w_kv, tm)
            inv = np.argsort(np.asarray(_MOBA_HEAD_PERM))
            q5 = qa.reshape(bs, ss, A_HEADS, A_HEAD_DIM)[:, :, inv].reshape(bs, ss, A_KV_HEADS, 2, A_HEAD_DIM)
            eye_kv = jnp.eye(A_KV_HEADS, dtype=F32)[None, :, None, None, :, None]
            q_rows = (q5.transpose(0, 2, 1, 3, 4)[:, :, :, :, None, :] * eye_kv).reshape(
                bs, A_KV_HEADS * ss * 2, A_KV_HEADS * A_HEAD_DIM)
            o_ms = _moba_sample(page_table, q_rows, ka.reshape(bs, ss, 256), va.reshape(bs, ss, 256),
                                ck_pages, cv_pages, e * n_pool)
            o_ms = o_ms.reshape(bs, ss, 2, A_KV_HEADS, A_HEAD_DIM).transpose(0, 1, 3, 2, 4).reshape(n_s, 512)
            q_abs = _matmul(qm, w_abs, tm, BF16).reshape(bs, ss * B_HEADS, 384)
            o_lat = _mla_sample(page_table, q_abs, mla.reshape(bs, ss, MLA_ROW), cm_pages, e * n_pool)
            o_mls = _matmul(o_lat.reshape(n_s, B_HEADS * MLA_KV_RANK), w_uv_bd, tm)
            xs = _mm_res_ln([o_ms, o_mls], [w_out_moba, w_out_mla], xs, ln_g[0], ln_b[0], tm)
            moba_k_s.append(ka.reshape(bs, ss, A_KV_HEADS, A_HEAD_DIM))
            moba_v_s.append(va.reshape(bs, ss, A_KV_HEADS, A_HEAD_DIM))
            mla_s.append(mla.reshape(bs, ss, MLA_ROW))
        else:
            o = l // 2
            w_odd = w_in_odd[o].astype(BF16)
            w_oo = w_out_odd[o].astype(BF16)
            dils = tuple(dil for _, dil in SWA_CONFIGS)
            hvs = _odd_proj(xp, tab_odd_p, sp // tm, w_odd, tm, bp, dils)
            outs, lses = [], []
            for g, (win, dil) in enumerate(SWA_CONFIGS):
                og, lg = _dilated_prompt(hvs[g])
                outs.append(og)
                lses.append(lg)
                keep = min(win, sp)
                tail = hvs[g][:, :, (sp - keep) // dil:, 512:1536]
                swa_p[g].append(tail.transpose(0, 2, 1, 3).reshape(bp, keep, 2, C_HEADS, C_HEAD_DIM))
            xp = _odd_out(outs, lses, w_oo, xp, ln_g[0], ln_b[0], tm)
            hs_g = _odd_proj(xs, tab_odd_s, n_s // tm, w_odd, tm, 1, (1, 1, 1))
            q_list = [a[0, 0, :, 0:512].reshape(bs, ss * C_HEADS, C_HEAD_DIM) for a in hs_g]
            kv_new = [a[0, 0, :, 512:1536].reshape(bs, ss * 2 * C_HEADS, C_HEAD_DIM) for a in hs_g]
            new_list = [jnp.pad(a, ((0, 0), (0, LANES - a.shape[1]), (0, 0))) for a in kv_new]
            outs_s, lses_s = _dilated_sample(q_list, new_list, swa_rows, o * bs)
            lses_s = [jnp.pad(a[:, :, 0].reshape(n_s, C_HEADS), ((0, 0), (0, LANES - C_HEADS))) for a in lses_s]
            xs = _odd_out([a.reshape(1, 1, n_s, 512) for a in outs_s], [a.reshape(1, 1, n_s, LANES) for a in lses_s],
                          w_oo, xs, ln_g[0], ln_b[0], tm)
            for g in range(N_SWA):
                swa_s[g].append(kv_new[g].reshape(bs, ss, 2, C_HEADS, C_HEAD_DIM))
        w_mq = w_mem_q[l].astype(BF16)
        w_mo = w_mem_o[l].astype(BF16)
        mkv = _matmul(mem2, w_mem_kv[l].astype(BF16), tm)
        mem_p.append(mkv.reshape(bp, N_MEM, 2, MEM_HEADS, MEM_HEAD_DIM))
        xp = _mem_prompt(xp, mkv.reshape(bp, N_MEM, 1024).astype(BF16), w_mq, w_mo,
                         ln_g[1], ln_b[1], sp // tm, tm)
        q_s = _matmul(xs, w_mq, tm).reshape(bs, ss * MEM_HEADS, MEM_HEAD_DIM)
        att_s = _mem_sample_attn(q_s, mem_rows, l * bs)
        xs = _mm_res_ln([att_s.reshape(n_s, 512)], [w_mo], xs, ln_g[1], ln_b[1], tm)
        wr_t = w_router[l].T
        rb = router_bias[l].reshape(N_EXPERTS, 1)
        sg, su, sd = w_sh_gate[l].astype(BF16), w_sh_up[l].astype(BF16), w_sh_down[l].astype(BF16)
        gate_p = _router(xp, wr_t, rb, 512)
        xp = _moe(xp, gate_p, l, w_exp_gate, w_exp_up, w_exp_down, sg, su, sd, ln_g[2], ln_b[2], 1024)
        gate_s = _router(xs, wr_t, rb, 512)
        xs = _moe(xs, gate_s, l, w_exp_gate, w_exp_up, w_exp_down, sg, su, sd, ln_g[2], ln_b[2], 512)

    return (xp.reshape(bp, sp, D_MODEL), xs.reshape(bs, ss, D_MODEL),
            jnp.stack(moba_k_p), jnp.stack(moba_v_p), jnp.stack(mla_p),
            jnp.stack(swa_p[0]), jnp.stack(swa_p[1]), jnp.stack(swa_p[2]), jnp.stack(mem_p),
            jnp.stack(moba_k_s), jnp.stack(moba_v_s), jnp.stack(mla_s),
            jnp.stack(swa_s[0]), jnp.stack(swa_s[1]), jnp.stack(swa_s[2]))
```
